```python
import jax, jax.numpy as jnp
from jax import lax
import numpy as np

D_MODEL = 1024
BATCH = 32
SEQ = 256
DEPTH = 2
DEC_BATCH = 8
DEC_SEQ = 4096
PAST_LEN = 256

GRID_W = 64
N_HEADS = 16
N_KV_HEADS = 4
HEAD_DIM = 64
GROUP = N_HEADS // N_KV_HEADS
Q_W = N_HEADS * HEAD_DIM
KV_W = N_KV_HEADS * HEAD_DIM
CONV_W = D_MODEL // 2
CONV_K = 3
IN_W = Q_W + 2 * KV_W + 3 * CONV_W + 2 * D_MODEL
N_EXPERTS = 16
EC_FACTOR = 2
D_EXPERT = 1024
Q_BLOCK = 128
AXIS_DIM = HEAD_DIM // 2
ROPE_THETA = 10000.0
EPS = 1e-6

kernel_name = "hybrid_conv_gqa_ec_diffusion_step"


def rmsnorm(x, g):
    x32 = x.astype(jnp.float32)
    y = x32 * lax.rsqrt(jnp.mean(x32 * x32, axis=-1, keepdims=True) + EPS)
    return y.astype(x.dtype) * g


def axial_rope(x, row, col):
    half = AXIS_DIM // 2
    inv_freq = ROPE_THETA ** (-jnp.arange(half, dtype=jnp.float32) / half)

    def rot(xa, pos):
        ang = pos.astype(jnp.float32)[:, None] * inv_freq[None, :]
        cos = jnp.cos(ang)[None, :, None, :].astype(x.dtype)
        sin = jnp.sin(ang)[None, :, None, :].astype(x.dtype)
        x1, x2 = xa[..., :half], xa[..., half:]
        return jnp.concatenate([x1 * cos - x2 * sin, x1 * sin + x2 * cos], axis=-1)

    return jnp.concatenate([rot(x[..., :AXIS_DIM], row), rot(x[..., AXIS_DIM:], col)], axis=-1)


def block_attention(q, k, v):
    b, n, _, hd = q.shape
    nblk = n // Q_BLOCK
    qb = q.reshape(b, nblk, Q_BLOCK, N_KV_HEADS, GROUP, hd).transpose(1, 0, 2, 3, 4, 5)
    scale = HEAD_DIM ** -0.5

    def one(qblk):
        s = jnp.einsum('bqkgd,bmkd->bkgqm', qblk, k).astype(jnp.float32) * scale
        p = jax.nn.softmax(s, axis=-1).astype(v.dtype)
        return jnp.einsum('bkgqm,bmkd->bqkgd', p, v)

    o = lax.map(one, qb)
    return o.transpose(1, 0, 2, 3, 4, 5).reshape(b, n, Q_W)


def dwconv3(u, w):
    up = jnp.pad(u, ((0, 0), (1, 1), (0, 0)))
    return w[0] * up[:, :-2] + w[1] * up[:, 1:-1] + w[2] * up[:, 2:]


def token_mixer(h, lp, grid_pos, ctx_k, ctx_v):
    b, n, _ = h.shape
    proj = h @ lp['w_in']
    offs = np.cumsum([Q_W, KV_W, KV_W, CONV_W, CONV_W, CONV_W, D_MODEL]).tolist()
    q, k, v, gb, gc, xin, ga_lin, gat_lin = jnp.split(proj, offs, axis=-1)
    q = rmsnorm(q.reshape(b, n, N_HEADS, HEAD_DIM), lp['q_norm'])
    k = rmsnorm(k.reshape(b, n, N_KV_HEADS, HEAD_DIM), lp['k_norm'])
    v = v.reshape(b, n, N_KV_HEADS, HEAD_DIM)
    if grid_pos is None:
        q_r, k_all, v_all = q, k, v
    else:
        row, col = grid_pos
        q_r = axial_rope(q, row, col)
        k_r = axial_rope(k, row, col)
        k_all = jnp.concatenate([ctx_k, k_r], axis=1)
        v_all = jnp.concatenate([ctx_v, v], axis=1)
    attn = block_attention(q_r, k_all, v_all) @ lp['w_attn_out']
    conv = (gb * dwconv3(gc * xin, lp['conv_w'])) @ lp['w_conv_out']
    merged = jax.nn.sigmoid(ga_lin) * conv + jax.nn.sigmoid(gat_lin) * attn
    return merged @ lp['w_o'], k, v


def expert_choice_ffn(h, lp):
    b, n, d = h.shape
    cap = max(1, EC_FACTOR * n // N_EXPERTS)
    logits = jnp.einsum('bnd,de->bne', h, lp['w_router']).astype(jnp.float32)
    aff = jax.nn.softmax(logits, axis=-1)
    vals, idx = lax.top_k(jnp.swapaxes(aff, 1, 2), cap)
    x_sel = jax.vmap(lambda hb, ib: hb[ib])(h, idx)
    gate = jnp.einsum('becd,edf->becf', x_sel, lp['w_gate'])
    up = jnp.einsum('becd,edf->becf', x_sel, lp['w_up'])
    y = jnp.einsum('becf,efd->becd', jax.nn.silu(gate) * up, lp['w_down'])
    y = y * vals[..., None].astype(h.dtype)
    return jax.vmap(lambda yb, ib: jnp.zeros((n, d), h.dtype).at[ib.reshape(-1)].add(yb.reshape(-1, d)))(y, idx)


def layer(x, mod, lp, grid_pos, ctx_k, ctx_v):
    sh1, sc1, g1, sh2, sc2, g2 = jnp.split(mod, 6, axis=-1)
    h = rmsnorm(x, lp['norm1']) * (1 + sc1) + sh1
    mix, k, v = token_mixer(h, lp, grid_pos, ctx_k, ctx_v)
    x = x + g1 * mix
    h2 = rmsnorm(x, lp['norm2']) * (1 + sc2) + sh2
    x = x + g2 * expert_choice_ffn(h2, lp)
    return x, k, v


def setup_inputs(seed: int = 0) -> dict:
    key = jax.random.key(seed)
    ks = jax.random.split(key, 24)
    f32 = jnp.float32
    nrm = lambda k, s, sc: (jax.random.normal(k, s, f32) * sc)
    return {
        'x_prompt': nrm(ks[0], (BATCH, SEQ, D_MODEL), 1.0),
        'x_sample': nrm(ks[1], (DEC_BATCH, DEC_SEQ, D_MODEL), 1.0),
        'cache_k': nrm(ks[2], (DEC_BATCH, DEPTH, PAST_LEN, N_KV_HEADS, HEAD_DIM), 1.0),
        'cache_v': nrm(ks[3], (DEC_BATCH, DEPTH, PAST_LEN, N_KV_HEADS, HEAD_DIM), 1.0),
        'c': nrm(ks[4], (DEC_BATCH, D_MODEL), 1.0),
        'c_ctx': nrm(ks[5], (D_MODEL,), 1.0),
        'w_mod': nrm(ks[6], (DEPTH, D_MODEL, 6 * D_MODEL), 0.5 * D_MODEL ** -0.5),
        'b_mod': nrm(ks[7], (DEPTH, 6 * D_MODEL), 0.02),
        'norm1': 1.0 + nrm(ks[8], (DEPTH, D_MODEL), 0.02),
        'norm2': 1.0 + nrm(ks[9], (DEPTH, D_MODEL), 0.02),
        'w_in': nrm(ks[10], (DEPTH, D_MODEL, IN_W), D_MODEL ** -0.5),
        'q_norm': 1.0 + nrm(ks[11], (DEPTH, HEAD_DIM), 0.02),
        'k_norm': 1.0 + nrm(ks[12], (DEPTH, HEAD_DIM), 0.02),
        'conv_w': nrm(ks[13], (DEPTH, CONV_K, CONV_W), CONV_K ** -0.5),
        'w_conv_out': nrm(ks[14], (DEPTH, CONV_W, D_MODEL), CONV_W ** -0.5),
        'w_attn_out': nrm(ks[15], (DEPTH, Q_W, D_MODEL), Q_W ** -0.5),
        'w_o': nrm(ks[16], (DEPTH, D_MODEL, D_MODEL), D_MODEL ** -0.5),
        'w_router': nrm(ks[17], (DEPTH, D_MODEL, N_EXPERTS), D_MODEL ** -0.5),
        'w_gate': nrm(ks[18], (DEPTH, N_EXPERTS, D_MODEL, D_EXPERT), D_MODEL ** -0.5),
        'w_up': nrm(ks[19], (DEPTH, N_EXPERTS, D_MODEL, D_EXPERT), D_MODEL ** -0.5),
        'w_down': nrm(ks[20], (DEPTH, N_EXPERTS, D_EXPERT, D_MODEL), D_EXPERT ** -0.5),
        'final_norm': 1.0 + nrm(ks[21], (D_MODEL,), 0.02),
    }


def reference(x_prompt, x_sample, cache_k, cache_v, c, c_ctx, w_mod, b_mod, norm1, norm2, w_in, q_norm, k_norm,
              conv_w, w_conv_out, w_attn_out, w_o, w_router, w_gate, w_up, w_down, final_norm):
    n_lat = x_sample.shape[1]
    rows = n_lat // GRID_W
    row = jnp.repeat(jnp.arange(rows, dtype=jnp.int32), GRID_W)
    col = jnp.tile(jnp.arange(GRID_W, dtype=jnp.int32), rows)

    xp = x_prompt
    xs = x_sample
    new_k, new_v = [], []
    for l in range(DEPTH):
        lp = {'norm1': norm1[l], 'norm2': norm2[l], 'w_in': w_in[l], 'q_norm': q_norm[l], 'k_norm': k_norm[l],
              'conv_w': conv_w[l], 'w_conv_out': w_conv_out[l], 'w_attn_out': w_attn_out[l], 'w_o': w_o[l],
              'w_router': w_router[l], 'w_gate': w_gate[l], 'w_up': w_up[l], 'w_down': w_down[l]}
        mod_ctx = (jax.nn.silu(c_ctx) @ w_mod[l] + b_mod[l])[None, None, :]
        xp, k_ctx, v_ctx = layer(xp, mod_ctx, lp, None, None, None)
        new_k.append(k_ctx)
        new_v.append(v_ctx)
        mod_lat = (jax.nn.silu(c) @ w_mod[l] + b_mod[l])[:, None, :]
        xs, _, _ = layer(xs, mod_lat, lp, (row, col), cache_k[:, l], cache_v[:, l])

    y_prompt = rmsnorm(xp, final_norm)
    y_sample = rmsnorm(xs, final_norm)
    new_cache_k = jnp.stack(new_k, axis=1)
    new_cache_v = jnp.stack(new_v, axis=1)
    return (y_prompt, y_sample, new_cache_k, new_cache_v)
```

```python
import functools

import jax
import jax.numpy as jnp
import numpy as np
from jax import lax
from jax.experimental import pallas as pl
from jax.experimental.pallas import tpu as pltpu

F32 = jnp.float32
BF16 = jnp.bfloat16
I32 = jnp.int32
U32 = jnp.uint32

D_MODEL = 1024
N_HEADS = 16
N_KV_HEADS = 4
HEAD_DIM = 64
GROUP = N_HEADS // N_KV_HEADS
Q_W = N_HEADS * HEAD_DIM
KV_W = N_KV_HEADS * HEAD_DIM
CONV_W = D_MODEL // 2
IN_W = Q_W + 2 * KV_W + 3 * CONV_W + 2 * D_MODEL
N_EXPERTS = 16
EC_FACTOR = 2
D_EXPERT = 1024
GRID_W = 64
AXIS_DIM = HEAD_DIM // 2
ROPE_THETA = 10000.0
EPS = 1e-6
MOD_ROWS = 16

LANES = 128
SUBLANES = 8
VMEM_LIMIT_BYTES = 56 * 1024 * 1024

TOKEN_TILE = 256
KEY_BLOCK = 512
CUMSUM_BLOCK = 256
RANK_BLOCK = 128
ROW_UNROLL = 8


def _cparams(sem):
    return pltpu.CompilerParams(dimension_semantics=sem, vmem_limit_bytes=VMEM_LIMIT_BYTES)


def _mm(a, b):
    return jnp.dot(a, b, preferred_element_type=F32)


def _mm_nt(a, b):
    return lax.dot_general(a, b, (((1,), (1,)), ((), ())), preferred_element_type=F32)


def _const_spec(shape):
    nd = len(shape)
    return pl.BlockSpec(shape, lambda *_: (0,) * nd)


def _mod_body(c_ref, w_ref, b_ref, o_ref):
    c = c_ref[...]
    s = c * jax.nn.sigmoid(c)
    o_ref[0] = _mm(s.astype(BF16), w_ref[0].astype(BF16)) + b_ref[0]


def _modulation(cmat, w_mod, b_mod):
    depth = w_mod.shape[0]
    tn = 1536
    return pl.pallas_call(
        _mod_body,
        out_shape=jax.ShapeDtypeStruct((depth, MOD_ROWS, 6 * D_MODEL), F32),
        grid=(depth, 6 * D_MODEL // tn),
        in_specs=[
            pl.BlockSpec((MOD_ROWS, D_MODEL), lambda l, j: (0, 0)),
            pl.BlockSpec((1, D_MODEL, tn), lambda l, j: (l, 0, j)),
            pl.BlockSpec((1, 1, tn), lambda l, j: (l, 0, j)),
        ],
        out_specs=pl.BlockSpec((1, MOD_ROWS, tn), lambda l, j: (l, 0, j)),
        compiler_params=_cparams(("arbitrary", "arbitrary")),
        name="modulation",
    )(cmat, w_mod, b_mod.reshape(depth, 1, 6 * D_MODEL))


def _rms(x):
    return x * lax.rsqrt(jnp.mean(x * x, axis=-1, keepdims=True) + EPS)


def _premix_body(*refs, is_lat, has_prev, tm):
    it = iter(refs)
    xa_ref = next(it)
    if has_prev:
        xb_ref = next(it)
        modp_ref = next(it)
    mod_ref = next(it)
    n1_ref = next(it)
    win_ref = next(it)
    gq_ref = next(it)
    gk_ref = next(it)
    bdq_ref = next(it)
    bdk_ref = next(it)
    if is_lat:
        cos_ref = next(it)
        sin_ref = next(it)
    q_ref = next(it)
    krep_ref = next(it)
    vt_ref = next(it)
    if not is_lat:
        kc_ref = next(it)
        vc_ref = next(it)
    u_ref = next(it)
    gb_ref = next(it)
    sga_ref = next(it)
    sgat_ref = next(it)
    if has_prev:
        xn_ref = next(it)

    d = D_MODEL
    x = xa_ref[0]
    if has_prev:
        x = x + modp_ref[0, :, 5 * d:6 * d] * xb_ref[0]
        xn_ref[0] = x
    sh1 = mod_ref[0, :, 0:d]
    sc1 = mod_ref[0, :, d:2 * d]
    h = (_rms(x) * n1_ref[...]) * (1.0 + sc1) + sh1
    hb = h.astype(BF16)

    def proj(lo, hi):
        return _mm(hb, win_ref[:, lo:hi])

    lane = lax.broadcasted_iota(I32, (tm, LANES), 1)
    first_half = (lane % AXIS_DIM) < (AXIS_DIM // 2)
    low_head = lane < HEAD_DIM

    def rope(chunk, c):
        if not is_lat:
            return chunk
        del c
        partner = jnp.where(first_half, pltpu.roll(chunk, LANES - AXIS_DIM // 2, 1),
                            pltpu.roll(chunk, AXIS_DIM // 2, 1))
        return chunk * cos_ref[...] + partner * sin_ref[...]

    pq = proj(0, Q_W)
    ssq = _mm((pq * pq).astype(BF16), bdq_ref[...])
    qn = pq * lax.rsqrt(ssq * (1.0 / HEAD_DIM) + EPS) * gq_ref[...]
    for c in range(Q_W // LANES):
        qc = rope(qn[:, c * LANES:(c + 1) * LANES], c)
        q_ref[0, :, c * LANES:(c + 1) * LANES] = (qc * (HEAD_DIM ** -0.5)).astype(BF16)

    pk = proj(Q_W, Q_W + KV_W)
    ssk = _mm((pk * pk).astype(BF16), bdk_ref[...])
    kn = pk * lax.rsqrt(ssk * (1.0 / HEAD_DIM) + EPS) * gk_ref[...]
    if not is_lat:
        kc_ref[0] = kn
    for c in range(KV_W // LANES):
        kc = rope(kn[:, c * LANES:(c + 1) * LANES], c)
        rolled = pltpu.roll(kc, HEAD_DIM, 1)
        even = jnp.where(low_head, kc, rolled).astype(BF16)
        odd = jnp.where(low_head, rolled, kc).astype(BF16)
        for s in range(KV_W // LANES):
            krep_ref[0, 2 * c, :, s * LANES:(s + 1) * LANES] = even
            krep_ref[0, 2 * c + 1, :, s * LANES:(s + 1) * LANES] = odd

    pv = proj(Q_W + KV_W, Q_W + 2 * KV_W)
    if not is_lat:
        vc_ref[0] = pv
    vt_ref[0, 0] = pv.T.astype(BF16)

    o = Q_W + 2 * KV_W
    gb_ref[0] = proj(o, o + CONV_W).astype(BF16)
    u_ref[0] = proj(o + CONV_W, o + 2 * CONV_W) * proj(o + 2 * CONV_W, o + 3 * CONV_W)
    o = o + 3 * CONV_W
    sga_ref[0] = jax.nn.sigmoid(proj(o, o + d)).astype(BF16)
    sgat_ref[0] = jax.nn.sigmoid(proj(o + d, o + 2 * d)).astype(BF16)


def _premix(xa, xb, modp, mod_l, row_of_batch, lw, consts, is_lat):
    b, n, d = xa.shape
    tm = TOKEN_TILE
    tk = min(KEY_BLOCK, n)
    has_prev = xb is not None
    sub = tk // tm
    tok_spec = lambda w: pl.BlockSpec((1, tm, w), lambda bi, i: (bi, i, 0))
    mod_spec = pl.BlockSpec((1, 1, 6 * d), lambda bi, i: (row_of_batch(bi), 0, 0))

    ins, specs = [xa], [tok_spec(d)]
    if has_prev:
        ins += [xb, modp]
        specs += [tok_spec(d), mod_spec]
    ins += [mod_l, lw["norm1"], lw["w_in"], lw["gq"], lw["gk"], consts["bdq"], consts["bdk"]]
    specs += [mod_spec, _const_spec((1, d)), _const_spec((d, IN_W)), _const_spec((1, Q_W)),
              _const_spec((1, KV_W)), _const_spec((Q_W, Q_W)), _const_spec((KV_W, KV_W))]
    if is_lat:
        ins += [consts["cos"], consts["sin"]]
        specs += [pl.BlockSpec((tm, LANES), lambda bi, i: (i, 0))] * 2

    outs = [jax.ShapeDtypeStruct((b, n, Q_W), BF16),
            jax.ShapeDtypeStruct((b, N_KV_HEADS, n, KV_W), BF16),
            jax.ShapeDtypeStruct((b, n // tk, KV_W, tk), BF16)]
    ospecs = [tok_spec(Q_W),
              pl.BlockSpec((1, N_KV_HEADS, tm, KV_W), lambda bi, i: (bi, 0, i, 0)),
              pl.BlockSpec((1, 1, KV_W, tm), lambda bi, i: (bi, i // sub, 0, i % sub))]
    if not is_lat:
        outs += [jax.ShapeDtypeStruct((b, n, KV_W), F32)] * 2
        ospecs += [tok_spec(KV_W)] * 2
    outs += [jax.ShapeDtypeStruct((b, n, CONV_W), F32), jax.ShapeDtypeStruct((b, n, CONV_W), BF16),
             jax.ShapeDtypeStruct((b, n, d), BF16), jax.ShapeDtypeStruct((b, n, d), BF16)]
    ospecs += [tok_spec(CONV_W), tok_spec(CONV_W), tok_spec(d), tok_spec(d)]
    if has_prev:
        outs.append(jax.ShapeDtypeStruct((b, n, d), F32))
        ospecs.append(tok_spec(d))

    res = pl.pallas_call(
        functools.partial(_premix_body, is_lat=is_lat, has_prev=has_prev, tm=tm),
        out_shape=outs, grid=(b, n // tm), in_specs=specs, out_specs=ospecs,
        compiler_params=_cparams(("arbitrary", "arbitrary")),
        name="premix_lat" if is_lat else "premix_ctx",
    )(*ins)
    return list(res)


def _attn_body(*refs, p_len, n, tq, tk):
    it = iter(refs)
    q_ref = next(it)
    if p_len:
        kc_ref = next(it)
        vc_ref = next(it)
    k_ref = next(it)
    v_ref = next(it)
    o_ref = next(it)
    kbd_ref = next(it)
    m_ref = next(it)
    l_ref = next(it)
    acc_ref = next(it)

    @pl.when(pl.program_id(2) == 0)
    def _build_block_diagonal_keys():
        def fill(src, dst0, rows):
            head_of_lane = lax.broadcasted_iota(I32, (rows, KV_W), 1) // HEAD_DIM
            for g in range(GROUP):
                kbd_ref[g, dst0:dst0 + rows, :] = jnp.where(head_of_lane == g, src, jnp.zeros_like(src))
        if p_len:
            fill(kc_ref[0, 0], 0, p_len)
        for j in range(n // tk):
            fill(k_ref[0, 0, j * tk:(j + 1) * tk, :], p_len + j * tk, tk)

    q = q_ref[0]
    m_ref[...] = jnp.full(m_ref.shape, -jnp.inf, F32)
    l_ref[...] = jnp.zeros(l_ref.shape, F32)
    acc_ref[...] = jnp.zeros(acc_ref.shape, F32)

    def key_block(start, size, vblk):
        for g in range(GROUP):
            kb = kbd_ref[g, pl.ds(start, size), :]
            s = _mm_nt(kb, q)
            m_prev = m_ref[g]
            m_new = jnp.maximum(m_prev, jnp.max(s, axis=0, keepdims=True))
            alpha = jnp.exp(m_prev - m_new)
            p = jnp.exp(s - m_new)
            l_ref[g] = alpha * l_ref[g] + jnp.sum(p, axis=0, keepdims=True)
            acc_ref[g] = alpha * acc_ref[g] + _mm(vblk, p.astype(BF16))
            m_ref[g] = m_new

    if p_len:
        key_block(0, p_len, vc_ref[0])

    def lat_block(i, carry):
        key_block(pl.multiple_of(p_len + i * tk, tk // 2), tk, v_ref[0, i])
        return carry

    lax.fori_loop(0, n // tk, lat_block, 0)
    out = jnp.concatenate([acc_ref[g] * (1.0 / l_ref[g]) for g in range(GROUP)], axis=0)
    o_ref[0] = out.T.astype(BF16)


def _attention(q, krep, vt, kc_rep=None, vc_t=None):
    b, n, _ = q.shape
    tk = vt.shape[-1]
    tq = TOKEN_TILE
    p_len = 0 if kc_rep is None else kc_rep.shape[2]
    ins, specs = [q], [pl.BlockSpec((1, tq, KV_W), lambda bi, h, i: (bi, i, h))]
    if p_len:
        ins += [kc_rep, vc_t]
        specs += [pl.BlockSpec((1, 1, p_len, KV_W), lambda bi, h, i: (bi, h, 0, 0)),
                  pl.BlockSpec((1, HEAD_DIM, p_len), lambda bi, h, i: (bi, h, 0))]
    ins += [krep, vt]
    specs += [pl.BlockSpec((1, 1, n, KV_W), lambda bi, h, i: (bi, h, 0, 0)),
              pl.BlockSpec((1, n // tk, HEAD_DIM, tk), lambda bi, h, i: (bi, 0, h, 0))]
    return pl.pallas_call(
        functools.partial(_attn_body, p_len=p_len, n=n, tq=tq, tk=tk),
        out_shape=jax.ShapeDtypeStruct((b, n, Q_W), BF16),
        grid=(b, N_KV_HEADS, n // tq),
        in_specs=specs,
        out_specs=pl.BlockSpec((1, tq, KV_W), lambda bi, h, i: (bi, i, h)),
        scratch_shapes=[pltpu.VMEM((GROUP, p_len + n, KV_W), BF16),
                        pltpu.VMEM((GROUP, 1, tq), F32),
                        pltpu.VMEM((GROUP, 1, tq), F32),
                        pltpu.VMEM((GROUP, HEAD_DIM, tq), F32)],
        compiler_params=_cparams(("arbitrary", "arbitrary", "arbitrary")),
        name="attention_lat" if p_len else "attention_ctx",
    )(*ins)


def _split_bf16(x):
    hi = x.astype(BF16)
    lo = (x - hi.astype(F32)).astype(BF16)
    return hi, lo


def _postmix_body(attn_ref, u_ref, up_ref, un_ref, gb_ref, sga_ref, sgat_ref, x_ref, mod_ref, n2_ref, cw_ref,
                  wa_ref, wc_ref, wo_ref, wr_ref, x1_ref, h2p_ref, affn_ref, afft_ref, *, tm):
    d = D_MODEL
    i = pl.program_id(1)
    last = pl.num_programs(1) - 1
    u = u_ref[0]
    row = lax.broadcasted_iota(I32, (tm, CONV_W), 0)
    prev_row = jnp.where(i > 0, up_ref[0, SUBLANES - 1:SUBLANES, :], 0.0)
    next_row = jnp.where(i < last, un_ref[0, 0:1, :], 0.0)
    u_m1 = jnp.where(row == 0, prev_row, pltpu.roll(u, 1, 0))
    u_p1 = jnp.where(row == tm - 1, next_row, pltpu.roll(u, tm - 1, 0))
    conv = cw_ref[0:1, :] * u_m1 + cw_ref[1:2, :] * u + cw_ref[2:3, :] * u_p1
    cv = (gb_ref[0].astype(F32) * conv).astype(BF16)
    conv_o = _mm(cv, wc_ref[...])
    attn_o = _mm(attn_ref[0], wa_ref[...])
    merged = sga_ref[0].astype(F32) * conv_o + sgat_ref[0].astype(F32) * attn_o
    mix = _mm(merged.astype(BF16), wo_ref[...])
    x1 = x_ref[0] + mod_ref[0, :, 2 * d:3 * d] * mix
    x1_ref[0] = x1
    h2 = (_rms(x1) * n2_ref[...]) * (1.0 + mod_ref[0, :, 4 * d:5 * d]) + mod_ref[0, :, 3 * d:4 * d]

    h_hi, h_lo = _split_bf16(h2)
    half = d // 2
    w_lo = lax.bitcast_convert_type(h_hi[:, :half].astype(F32), U32) >> 16
    w_hi = lax.bitcast_convert_type(h_hi[:, half:].astype(F32), U32) & jnp.uint32(0xFFFF0000)
    h2p_ref[0] = w_lo | w_hi

    r_hi, r_lo = _split_bf16(wr_ref[...])
    logits = (_mm(h_hi, r_hi) + _mm(h_hi, r_lo)
              + _mm(h_lo, r_hi))
    lane = lax.broadcasted_iota(I32, (tm, LANES), 1)
    valid = lane < N_EXPERTS
    logits = jnp.where(valid, logits, -jnp.inf)
    e = jnp.exp(logits - jnp.max(logits, axis=-1, keepdims=True))
    e = jnp.where(valid, e, 0.0)
    aff = e / jnp.sum(e, axis=-1, keepdims=True)
    affn_ref[0] = aff
    afft_ref[0] = aff.T[0:N_EXPERTS, :]


def _postmix(attn, u, gb, sga, sgat, x, mod_l, row_of_batch, lw):
    b, n, d = x.shape
    tm = TOKEN_TILE
    nsub = n // SUBLANES
    per = tm // SUBLANES
    tok_spec = lambda w: pl.BlockSpec((1, tm, w), lambda bi, i: (bi, i, 0))
    specs = [tok_spec(Q_W), tok_spec(CONV_W),
             pl.BlockSpec((1, SUBLANES, CONV_W), lambda bi, i: (bi, jnp.maximum(i * per - 1, 0), 0)),
             pl.BlockSpec((1, SUBLANES, CONV_W), lambda bi, i: (bi, jnp.minimum((i + 1) * per, nsub - 1), 0)),
             tok_spec(CONV_W), tok_spec(d), tok_spec(d), tok_spec(d),
             pl.BlockSpec((1, 1, 6 * d), lambda bi, i: (row_of_batch(bi), 0, 0)),
             _const_spec((1, d)), _const_spec((SUBLANES, CONV_W)),
             _const_spec((Q_W, d)), _const_spec((CONV_W, d)), _const_spec((d, d)), _const_spec((d, LANES))]
    outs = [jax.ShapeDtypeStruct((b, n, d), F32), jax.ShapeDtypeStruct((b, n, d // 2), U32),
            jax.ShapeDtypeStruct((b, n, LANES), F32), jax.ShapeDtypeStruct((b, N_EXPERTS, n), F32)]
    ospecs = [tok_spec(d), tok_spec(d // 2), tok_spec(LANES),
              pl.BlockSpec((1, N_EXPERTS, tm), lambda bi, i: (bi, 0, i))]
    return pl.pallas_call(
        functools.partial(_postmix_body, tm=tm),
        out_shape=outs, grid=(b, n // tm), in_specs=specs, out_specs=ospecs,
        compiler_params=_cparams(("arbitrary", "arbitrary")),
        name="postmix",
    )(attn, u, u, u, gb, sga, sgat, x, mod_l, lw["norm2"], lw["conv_w"], lw["w_attn_out"], lw["w_conv_out"],
      lw["w_o"], lw["w_router"])


def _topk_body(aff_ref, ut_ref, idx_ref, pos_ref, *, n, cap):
    a = aff_ref[0]
    bits = lax.bitcast_convert_type(a, I32)
    ne = N_EXPERTS

    def count(mask):
        return jnp.sum(jnp.where(mask, 1.0, 0.0), axis=1, keepdims=True)

    def thr_step(t, thr):
        cand = thr | (jnp.int32(1) << (30 - t))
        return jnp.where(count(bits >= cand) >= cap, cand, thr)

    thr = lax.fori_loop(0, 31, thr_step, jnp.zeros((ne, 1), I32))
    gt = bits > thr
    eq = bits == thr
    need = cap - count(gt)
    tok = lax.broadcasted_iota(I32, (ne, n), 1)
    nbits = int(np.log2(n)) + 1

    def tie_step(t, bound):
        cand = bound + (jnp.int32(1) << (nbits - 1 - t))
        ok = (cand <= n) & (count(eq & (tok < cand)) <= need)
        return jnp.where(ok, cand, bound)

    bound = lax.fori_loop(0, nbits, tie_step, jnp.zeros((ne, 1), I32))
    sel = jnp.where(gt | (eq & (tok < bound)), 1.0, 0.0)

    cb = min(CUMSUM_BLOCK, n)
    nblk = n // cb
    stacked = jnp.concatenate([sel[:, k * cb:(k + 1) * cb] for k in range(nblk)], axis=0).astype(BF16)
    within = _mm(stacked, ut_ref[...])
    off = jnp.zeros((ne, 1), F32)
    for k in range(nblk):
        blk = within[k * ne:(k + 1) * ne, :]
        pos = blk + off
        for ex in range(ne):
            pos_ref[ex, :, k * cb:(k + 1) * cb] = jnp.broadcast_to(pos[ex:ex + 1, :], (SUBLANES, cb))
        off = off + blk[:, cb - 1:cb]

    nrb = max(cap // RANK_BLOCK, 1)
    tiles = RANK_BLOCK // SUBLANES
    lane = lax.broadcasted_iota(I32, (RANK_BLOCK, LANES), 1)
    rank0 = (lax.broadcasted_iota(I32, (tiles, SUBLANES, LANES), 0) * SUBLANES
             + lax.broadcasted_iota(I32, (tiles, SUBLANES, LANES), 1)).astype(F32)

    def expert_step(ei, cols):
        for rb in range(nrb):
            rank = rank0 + float(rb * RANK_BLOCK)
            cnt = jnp.zeros((tiles, SUBLANES, LANES), F32)
            for lc in range(n // LANES):
                prow = pos_ref[ei, :, lc * LANES:(lc + 1) * LANES]
                cnt = cnt + jnp.where(prow[None] <= rank, 1.0, 0.0)
            col = jnp.sum(cnt.reshape(RANK_BLOCK, LANES), axis=1, keepdims=True)
            cols = jnp.where(lane == ei * nrb + rb, col, cols)
        return cols

    cols = lax.fori_loop(0, ne, expert_step, jnp.zeros((RANK_BLOCK, LANES), F32))
    rows = cols.T
    width = min(cap, RANK_BLOCK)
    idx_ref[0] = rows[0:ne * nrb, 0:width].astype(I32)


def _topk(afft, consts):
    b, ne, n = afft.shape
    cap = max(1, EC_FACTOR * n // N_EXPERTS)
    nrb = max(cap // RANK_BLOCK, 1)
    width = min(cap, RANK_BLOCK)
    cb = min(CUMSUM_BLOCK, n)
    idx = pl.pallas_call(
        functools.partial(_topk_body, n=n, cap=cap),
        out_shape=jax.ShapeDtypeStruct((b, ne * nrb, width), I32),
        grid=(b,),
        in_specs=[pl.BlockSpec((1, ne, n), lambda bi: (bi, 0, 0)), _const_spec((cb, cb))],
        out_specs=pl.BlockSpec((1, ne * nrb, width), lambda bi: (bi, 0, 0)),
        scratch_shapes=[pltpu.VMEM((ne, SUBLANES, n), F32)],
        compiler_params=_cparams(("arbitrary",)),
        name="topk",
    )(afft, consts["ut"][:cb, :cb])
    return idx.reshape(b, ne, cap)


def _ffn_body(idx_ref, h2p_ref, affn_ref, wg_ref, wu_ref, wd_ref, y_ref, xs_ref, ws_ref, *, rows, e_axis):
    def gather(j, carry):
        for k in range(ROW_UNROLL):
            r = j * ROW_UNROLL + k
            t = idx_ref[0, 0, r]
            xs_ref[pl.ds(r, 1), :] = h2p_ref[0, pl.ds(t, 1), :]
            ws_ref[pl.ds(r, 1), :] = affn_ref[0, pl.ds(t, 1), :]
        return carry

    lax.fori_loop(0, rows // ROW_UNROLL, gather, 0)
    words = xs_ref[...]
    x_lo = lax.bitcast_convert_type(words << 16, F32).astype(BF16)
    x_hi = lax.bitcast_convert_type(words & jnp.uint32(0xFFFF0000), F32).astype(BF16)
    half = D_MODEL // 2

    def up_proj(w_ref):
        return (_mm(x_lo, w_ref[0, 0:half, :])
                + _mm(x_hi, w_ref[0, half:, :]))

    gate = up_proj(wg_ref)
    hidden = ((gate * jax.nn.sigmoid(gate)) * up_proj(wu_ref)).astype(BF16)
    y = _mm(hidden, wd_ref[0])
    lane = lax.broadcasted_iota(I32, (rows, LANES), 1)
    val = jnp.sum(jnp.where(lane == pl.program_id(e_axis), ws_ref[...], 0.0), axis=1, keepdims=True)
    y_ref[0, 0] = y * val


def _moe_ffn(idx_steps, h2p, affn, lw, group, expert_outer):
    ng, ne, rows = idx_steps.shape
    gtok = h2p.shape[1] * group
    h2p = h2p.reshape(ng, gtok, D_MODEL // 2)
    affn = affn.reshape(ng, gtok, LANES)
    if expert_outer:
        grid = (ne, ng)
        ge = lambda a, b_: (b_, a)
    else:
        grid = (ng, ne)
        ge = lambda a, b_: (a, b_)
    g_of = lambda a, b_: ge(a, b_)[0]
    e_of = lambda a, b_: ge(a, b_)[1]
    idx_flat = idx_steps.reshape(ng * ne, 1, rows)
    w_spec = lambda r, c: pl.BlockSpec((1, r, c), lambda a, b_: (e_of(a, b_), 0, 0))
    return pl.pallas_call(
        functools.partial(_ffn_body, rows=rows, e_axis=0 if expert_outer else 1),
        out_shape=jax.ShapeDtypeStruct((ng, ne, rows, D_MODEL), F32),
        grid=grid,
        in_specs=[pl.BlockSpec((1, 1, rows), lambda a, b_: (g_of(a, b_) * ne + e_of(a, b_), 0, 0),
                               memory_space=pltpu.SMEM),
                  pl.BlockSpec((1, gtok, D_MODEL // 2), lambda a, b_: (g_of(a, b_), 0, 0)),
                  pl.BlockSpec((1, gtok, LANES), lambda a, b_: (g_of(a, b_), 0, 0)),
                  w_spec(D_MODEL, D_EXPERT), w_spec(D_MODEL, D_EXPERT), w_spec(D_EXPERT, D_MODEL)],
        out_specs=pl.BlockSpec((1, 1, rows, D_MODEL), lambda a, b_: (g_of(a, b_), e_of(a, b_), 0, 0)),
        scratch_shapes=[pltpu.VMEM((rows, D_MODEL // 2), U32), pltpu.VMEM((rows, LANES), F32)],
        compiler_params=_cparams(("arbitrary", "arbitrary")),
        name="moe_ffn",
    )(idx_flat, h2p, affn, lw["w_gate"], lw["w_up"], lw["w_down"])


def _combine_body(idx_ref, y_ref, o_ref, *, rows):
    @pl.when(pl.program_id(1) == 0)
    def _zero():
        o_ref[...] = jnp.zeros(o_ref.shape, F32)

    def scatter(j, carry):
        toks = [idx_ref[0, 0, j * ROW_UNROLL + k] for k in range(ROW_UNROLL)]
        vals = [o_ref[0, pl.ds(toks[k], 1), :] + y_ref[0, 0, pl.ds(j * ROW_UNROLL + k, 1), :]
                for k in range(ROW_UNROLL)]
        for k in range(ROW_UNROLL):
            o_ref[0, pl.ds(toks[k], 1), :] = vals[k]
        return carry

    lax.fori_loop(0, rows // ROW_UNROLL, scatter, 0)


def _combine(idx_steps, ysel, gtok):
    ng, ne, rows = idx_steps.shape
    return pl.pallas_call(
        functools.partial(_combine_body, rows=rows),
        out_shape=jax.ShapeDtypeStruct((ng, gtok, D_MODEL), F32),
        grid=(ng, ne),
        in_specs=[pl.BlockSpec((1, 1, rows), lambda g, e: (g * ne + e, 0, 0), memory_space=pltpu.SMEM),
                  pl.BlockSpec((1, 1, rows, D_MODEL), lambda g, e: (g, e, 0, 0))],
        out_specs=pl.BlockSpec((1, gtok, D_MODEL), lambda g, e: (g, 0, 0)),
        compiler_params=_cparams(("arbitrary", "arbitrary")),
        name="moe_combine",
    )(idx_steps.reshape(ng * ne, 1, rows), ysel)


def _expert_choice(h2p, affn, afft, lw, consts, group, expert_outer):
    b, n, _ = h2p.shape
    idx = _topk(afft, consts)
    cap = idx.shape[-1]
    ng = b // group
    offs = (jnp.arange(b, dtype=I32) % group * n).reshape(ng, group, 1, 1)
    idx_steps = (idx.reshape(ng, group, N_EXPERTS, cap) + offs).transpose(0, 2, 1, 3).reshape(ng, N_EXPERTS, group * cap)
    ysel = _moe_ffn(idx_steps, h2p, affn, lw, group, expert_outer)
    moe = _combine(idx_steps, ysel, group * n)
    return moe.reshape(b, n, D_MODEL)


def _final_body(x1_ref, moe_ref, mod_ref, g_ref, o_ref):
    d = D_MODEL
    x = x1_ref[0] + mod_ref[0, :, 5 * d:6 * d] * moe_ref[0]
    o_ref[0] = _rms(x) * g_ref[...]


def _final_norm(x1, moe, mod_l, row_of_batch, g):
    b, n, d = x1.shape
    tm = TOKEN_TILE
    tok = pl.BlockSpec((1, tm, d), lambda bi, i: (bi, i, 0))
    return pl.pallas_call(
        _final_body,
        out_shape=jax.ShapeDtypeStruct((b, n, d), F32),
        grid=(b, n // tm),
        in_specs=[tok, tok, pl.BlockSpec((1, 1, 6 * d), lambda bi, i: (row_of_batch(bi), 0, 0)), _const_spec((1, d))],
        out_specs=tok,
        compiler_params=_cparams(("arbitrary", "arbitrary")),
        name="final_norm",
    )(x1, moe, mod_l, g)


def _rope_tables(n_lat):
    half = AXIS_DIM // 2
    inv_freq = ROPE_THETA ** (-jnp.arange(half, dtype=F32) / half)
    t = jnp.arange(n_lat, dtype=jnp.int32)
    row = (t // GRID_W).astype(F32)
    col = (t % GRID_W).astype(F32)
    lane = np.arange(LANES)
    dim = lane % HEAD_DIM
    use_col = jnp.asarray(dim >= AXIS_DIM)
    freq = inv_freq[jnp.asarray(dim % half)]
    pos = jnp.where(use_col[None, :], col[:, None], row[:, None])
    ang = pos * freq[None, :]
    sign = jnp.asarray(np.where(dim % AXIS_DIM < half, -1.0, 1.0), dtype=F32)
    return jnp.cos(ang), jnp.sin(ang) * sign[None, :]


def _block_diag_ones(width):
    seg = np.arange(width) // HEAD_DIM
    return jnp.asarray(seg[:, None] == seg[None, :], dtype=BF16)


def kernel(x_prompt, x_sample, cache_k, cache_v, c, c_ctx, w_mod, b_mod, norm1, norm2, w_in, q_norm, k_norm,
           conv_w, w_conv_out, w_attn_out, w_o, w_router, w_gate, w_up, w_down, final_norm):
    depth = w_mod.shape[0]
    bc, nc, d = x_prompt.shape
    bl, nl, _ = x_sample.shape
    p_len = cache_k.shape[2]
    assert d == D_MODEL and bl + 1 <= MOD_ROWS
    assert nc % TOKEN_TILE == 0 and nl % KEY_BLOCK == 0 and p_len % TOKEN_TILE == 0

    cmat = jnp.zeros((MOD_ROWS, d), F32).at[0].set(c_ctx).at[1:1 + bl].set(c)
    mods = _modulation(cmat, w_mod, b_mod).reshape(depth, MOD_ROWS, 1, 6 * d)

    cos_t, sin_t = _rope_tables(nl)
    ut = np.arange(CUMSUM_BLOCK)
    consts = {"bdq": _block_diag_ones(Q_W), "bdk": _block_diag_ones(KV_W), "cos": cos_t, "sin": sin_t,
              "ut": jnp.asarray(ut[:, None] <= ut[None, :], dtype=BF16)}

    kc_rep = jnp.tile(cache_k.transpose(0, 1, 3, 2, 4), (1, 1, 1, 1, GROUP)).astype(BF16)
    vc_t = cache_v.transpose(0, 1, 3, 4, 2).reshape(bl, depth, KV_W, p_len).astype(BF16)

    ctx_row = lambda bi: 0
    lat_row = lambda bi: bi + 1
    ctx_group = min(8, bc)

    state = {"ctx": (x_prompt, None), "lat": (x_sample, None)}
    new_k, new_v = [], []
    for l in range(depth):
        lw = {
            "norm1": norm1[l].reshape(1, d), "norm2": norm2[l].reshape(1, d),
            "w_in": w_in[l].astype(BF16),
            "gq": jnp.tile(q_norm[l], N_HEADS).reshape(1, Q_W), "gk": jnp.tile(k_norm[l], N_KV_HEADS).reshape(1, KV_W),
            "conv_w": jnp.zeros((SUBLANES, CONV_W), F32).at[0:3].set(conv_w[l]),
            "w_conv_out": w_conv_out[l].astype(BF16), "w_attn_out": w_attn_out[l].astype(BF16),
            "w_o": w_o[l].astype(BF16),
            "w_router": jnp.zeros((d, LANES), F32).at[:, 0:N_EXPERTS].set(w_router[l]),
            "w_gate": w_gate[l].astype(BF16), "w_up": w_up[l].astype(BF16), "w_down": w_down[l].astype(BF16),
        }
        mod_l = mods[l]
        modp = mods[l - 1] if l else None
        for name in ("ctx", "lat"):
            is_lat = name == "lat"
            row_fn = lat_row if is_lat else ctx_row
            xa, xb = state[name]
            res = _premix(xa, xb, modp if xb is not None else None, mod_l, row_fn, lw, consts, is_lat)
            q, krep, vt = res[0:3]
            pos = 3
            if not is_lat:
                new_k.append(res[3])
                new_v.append(res[4])
                pos = 5
            u, gb, sga, sgat = res[pos:pos + 4]
            x_cur = res[pos + 4] if xb is not None else xa
            if is_lat:
                attn = _attention(q, krep, vt, kc_rep[:, l], vc_t[:, l])
            else:
                attn = _attention(q, krep, vt)
            x1, h2p, affn, afft = _postmix(attn, u, gb, sga, sgat, x_cur, mod_l, row_fn, lw)
            moe = _expert_choice(h2p, affn, afft, lw, consts, group=1 if is_lat else ctx_group,
                                 expert_outer=not is_lat)
            state[name] = (x1, moe)

    g = final_norm.reshape(1, d)
    y_prompt = _final_norm(*state["ctx"], mods[depth - 1], ctx_row, g)
    y_sample = _final_norm(*state["lat"], mods[depth - 1], lat_row, g)
    shape_kv = (bc, depth, nc, N_KV_HEADS, HEAD_DIM)
    new_cache_k = jnp.stack(new_k, axis=1).reshape(shape_kv)
    new_cache_v = jnp.stack(new_v, axis=1).reshape(shape_kv)
    return (y_prompt, y_sample, new_cache_k, new_cache_v)
```

```python
import functools

import jax
import jax.numpy as jnp
import numpy as np
from jax import lax
from jax.experimental import pallas as pl
from jax.experimental.pallas import tpu as pltpu

F32 = jnp.float32
BF16 = jnp.bfloat16
I32 = jnp.int32
U32 = jnp.uint32

D_MODEL = 1024
N_HEADS = 16
N_KV_HEADS = 4
HEAD_DIM = 64
GROUP = N_HEADS // N_KV_HEADS
Q_W = N_HEADS * HEAD_DIM
KV_W = N_KV_HEADS * HEAD_DIM
CONV_W = D_MODEL // 2
IN_W = Q_W + 2 * KV_W + 3 * CONV_W + 2 * D_MODEL
N_EXPERTS = 16
EC_FACTOR = 2
D_EXPERT = 1024
GRID_W = 64
AXIS_DIM = HEAD_DIM // 2
ROPE_THETA = 10000.0
EPS = 1e-6
MOD_ROWS = 16

LANES = 128
SUBLANES = 8
VMEM_LIMIT_BYTES = 56 * 1024 * 1024

TOKEN_TILE = 256
KEY_BLOCK = 1024
CUMSUM_BLOCK = 256
RANK_BLOCK = 128
TOPK_ROWS = 512 * 1024
ROW_UNROLL = 8
ONES_ROWS = 16
LOG2_E = 1.4426950408889634


def _cparams(sem):
    return pltpu.CompilerParams(dimension_semantics=sem, vmem_limit_bytes=VMEM_LIMIT_BYTES)


def _mm(a, b):
    return jnp.dot(a, b, preferred_element_type=F32)


def _const_spec(shape):
    nd = len(shape)
    return pl.BlockSpec(shape, lambda *_: (0,) * nd)


def _mod_body(c_ref, w_ref, b_ref, o_ref):
    c = c_ref[...]
    s = c * jax.nn.sigmoid(c)
    o_ref[0] = _mm(s.astype(BF16), w_ref[0].astype(BF16)) + b_ref[0]


def _modulation(cmat, w_mod, b_mod):
    depth = w_mod.shape[0]
    tn = 1536
    return pl.pallas_call(
        _mod_body,
        out_shape=jax.ShapeDtypeStruct((depth, MOD_ROWS, 6 * D_MODEL), F32),
        grid=(depth, 6 * D_MODEL // tn),
        in_specs=[
            pl.BlockSpec((MOD_ROWS, D_MODEL), lambda l, j: (0, 0)),
            pl.BlockSpec((1, D_MODEL, tn), lambda l, j: (l, 0, j)),
            pl.BlockSpec((1, 1, tn), lambda l, j: (l, 0, j)),
        ],
        out_specs=pl.BlockSpec((1, MOD_ROWS, tn), lambda l, j: (l, 0, j)),
        compiler_params=_cparams(("arbitrary", "arbitrary")),
        name="modulation",
    )(cmat, w_mod, b_mod.reshape(depth, 1, 6 * D_MODEL))


def _rms(x):
    return x * lax.rsqrt(jnp.mean(x * x, axis=-1, keepdims=True) + EPS)


def _premix_body(*refs, is_lat, has_prev, tm):
    it = iter(refs)
    xa_ref = next(it)
    if has_prev:
        xb_ref = next(it)
        modp_ref = next(it)
    mod_ref = next(it)
    n1_ref = next(it)
    win_ref = next(it)
    gq_ref = next(it)
    gk_ref = next(it)
    bdq_ref = next(it)
    bdk_ref = next(it)
    if is_lat:
        cos_ref = next(it)
        sin_ref = next(it)
    q_ref = next(it)
    krep_ref = next(it)
    vt_ref = next(it)
    if not is_lat:
        kc_ref = next(it)
        vc_ref = next(it)
    u_ref = next(it)
    gb_ref = next(it)
    sga_ref = next(it)
    sgat_ref = next(it)
    if has_prev:
        xn_ref = next(it)

    d = D_MODEL
    x = xa_ref[0]
    if has_prev:
        x = x + modp_ref[0, :, 5 * d:6 * d] * xb_ref[0]
        xn_ref[0] = x
    sh1 = mod_ref[0, :, 0:d]
    sc1 = mod_ref[0, :, d:2 * d]
    h = (_rms(x) * n1_ref[...]) * (1.0 + sc1) + sh1
    hb = h.astype(BF16)

    def proj(lo, hi):
        return _mm(hb, win_ref[:, lo:hi])

    lane = lax.broadcasted_iota(I32, (tm, LANES), 1)
    first_half = (lane % AXIS_DIM) < (AXIS_DIM // 2)
    low_head = lane < HEAD_DIM

    def rope(chunk, c):
        if not is_lat:
            return chunk
        del c
        partner = jnp.where(first_half, pltpu.roll(chunk, LANES - AXIS_DIM // 2, 1),
                            pltpu.roll(chunk, AXIS_DIM // 2, 1))
        return chunk * cos_ref[...] + partner * sin_ref[...]

    pq = proj(0, Q_W)
    ssq = _mm((pq * pq).astype(BF16), bdq_ref[...])
    qn = pq * lax.rsqrt(ssq * (1.0 / HEAD_DIM) + EPS) * gq_ref[...]
    for c in range(Q_W // LANES):
        qc = rope(qn[:, c * LANES:(c + 1) * LANES], c)
        q_ref[0, :, c * LANES:(c + 1) * LANES] = (qc * (HEAD_DIM ** -0.5 * LOG2_E)).astype(BF16)

    pk = proj(Q_W, Q_W + KV_W)
    ssk = _mm((pk * pk).astype(BF16), bdk_ref[...])
    kn = pk * lax.rsqrt(ssk * (1.0 / HEAD_DIM) + EPS) * gk_ref[...]
    if not is_lat:
        kc_ref[0] = kn
    for c in range(KV_W // LANES):
        kc = rope(kn[:, c * LANES:(c + 1) * LANES], c)
        rolled = pltpu.roll(kc, HEAD_DIM, 1)
        even = jnp.where(low_head, kc, rolled).astype(BF16)
        odd = jnp.where(low_head, rolled, kc).astype(BF16)
        for s in range(KV_W // LANES):
            krep_ref[0, 2 * c, :, s * LANES:(s + 1) * LANES] = even
            krep_ref[0, 2 * c + 1, :, s * LANES:(s + 1) * LANES] = odd

    pv = proj(Q_W + KV_W, Q_W + 2 * KV_W)
    if not is_lat:
        vc_ref[0] = pv
    vt_ref[0, 0] = pv.T.astype(BF16)

    o = Q_W + 2 * KV_W
    gb_ref[0] = proj(o, o + CONV_W).astype(BF16)
    u_ref[0] = proj(o + CONV_W, o + 2 * CONV_W) * proj(o + 2 * CONV_W, o + 3 * CONV_W)
    o = o + 3 * CONV_W
    sga_ref[0] = jax.nn.sigmoid(proj(o, o + d)).astype(BF16)
    sgat_ref[0] = jax.nn.sigmoid(proj(o + d, o + 2 * d)).astype(BF16)


def _premix(xa, xb, modp, mod_l, row_of_batch, lw, consts, is_lat):
    b, n, d = xa.shape
    tm = TOKEN_TILE
    tk = min(KEY_BLOCK, n)
    has_prev = xb is not None
    sub = tk // tm
    tok_spec = lambda w: pl.BlockSpec((1, tm, w), lambda bi, i: (bi, i, 0))
    mod_spec = pl.BlockSpec((1, 1, 6 * d), lambda bi, i: (row_of_batch(bi), 0, 0))

    ins, specs = [xa], [tok_spec(d)]
    if has_prev:
        ins += [xb, modp]
        specs += [tok_spec(d), mod_spec]
    ins += [mod_l, lw["norm1"], lw["w_in"], lw["gq"], lw["gk"], consts["bdq"], consts["bdk"]]
    specs += [mod_spec, _const_spec((1, d)), _const_spec((d, IN_W)), _const_spec((1, Q_W)),
              _const_spec((1, KV_W)), _const_spec((Q_W, Q_W)), _const_spec((KV_W, KV_W))]
    if is_lat:
        ins += [consts["cos"], consts["sin"]]
        specs += [pl.BlockSpec((tm, LANES), lambda bi, i: (i, 0))] * 2

    outs = [jax.ShapeDtypeStruct((b, n, Q_W), BF16),
            jax.ShapeDtypeStruct((b, N_KV_HEADS, n, KV_W), BF16),
            jax.ShapeDtypeStruct((b, n // tk, KV_W, tk), BF16)]
    ospecs = [tok_spec(Q_W),
              pl.BlockSpec((1, N_KV_HEADS, tm, KV_W), lambda bi, i: (bi, 0, i, 0)),
              pl.BlockSpec((1, 1, KV_W, tm), lambda bi, i: (bi, i // sub, 0, i % sub))]
    if not is_lat:
        outs += [jax.ShapeDtypeStruct((b, n, KV_W), F32)] * 2
        ospecs += [tok_spec(KV_W)] * 2
    outs += [jax.ShapeDtypeStruct((b, n, CONV_W), F32), jax.ShapeDtypeStruct((b, n, CONV_W), BF16),
             jax.ShapeDtypeStruct((b, n, d), BF16), jax.ShapeDtypeStruct((b, n, d), BF16)]
    ospecs += [tok_spec(CONV_W), tok_spec(CONV_W), tok_spec(d), tok_spec(d)]
    if has_prev:
        outs.append(jax.ShapeDtypeStruct((b, n, d), F32))
        ospecs.append(tok_spec(d))

    res = pl.pallas_call(
        functools.partial(_premix_body, is_lat=is_lat, has_prev=has_prev, tm=tm),
        out_shape=outs, grid=(b, n // tm), in_specs=specs, out_specs=ospecs,
        compiler_params=_cparams(("arbitrary", "arbitrary")),
        name="premix_lat" if is_lat else "premix_ctx",
    )(*ins)
    return list(res)


def _attn_body(*refs, p_len, n, tq, tk):
    it = iter(refs)
    q_ref = next(it)
    if p_len:
        kc_ref = next(it)
        vc_ref = next(it)
    k_ref = next(it)
    v_ref = next(it)
    o_ref = next(it)
    qbd_ref = next(it)
    s_ref = next(it)
    m_ref = next(it)
    acc_ref = next(it)

    qt = q_ref[0].astype(F32).T
    head_of_row = lax.broadcasted_iota(I32, (KV_W, tq), 0) // HEAD_DIM
    for g in range(GROUP):
        qbd_ref[g] = jnp.where(head_of_row == g, qt, 0.0).astype(BF16)
    m_ref[...] = jnp.full(m_ref.shape, -jnp.inf, F32)
    acc_ref[...] = jnp.zeros(acc_ref.shape, F32)

    def score(k_rows, g, size):
        s = _mm(k_rows, qbd_ref[g])
        s_ref[g, 0:size, :] = s
        return jnp.max(s, axis=0, keepdims=True)

    def softmax_pv(v_blk, g, size, m_blk):
        m_prev = m_ref[g]
        m_new = jnp.maximum(m_prev, m_blk)
        alpha = jnp.exp2(m_prev - m_new)
        p = jnp.exp2(s_ref[g, 0:size, :] - m_new).astype(BF16)
        v_ext = jnp.concatenate([v_blk, jnp.ones((ONES_ROWS, size), BF16)], axis=0)
        acc_ref[g] = alpha * acc_ref[g] + _mm(v_ext, p)
        m_ref[g] = m_new

    def key_block(get_k, get_v, size, m_blk, next_k, next_size):
        for g in range(GROUP):
            if g + 1 < GROUP:
                m_next = score(get_k(), g + 1, size)
            elif next_k is not None:
                m_next = score(next_k(), 0, next_size)
            else:
                m_next = None
            softmax_pv(get_v(), g, size, m_blk)
            m_blk = m_next
        return m_blk

    def lat_k(i):
        if isinstance(i, int):
            return lambda: k_ref[0, 0, i * tk:(i + 1) * tk, :]
        return lambda: k_ref[0, 0, pl.ds(pl.multiple_of(i * tk, tk), tk), :]

    def lat_v(i):
        return lambda: v_ref[0, i]

    nb = n // tk
    if p_len:
        cache_k = lambda: kc_ref[0, 0]
        m_blk = score(cache_k(), 0, p_len)
        m_blk = key_block(cache_k, lambda: vc_ref[0], p_len, m_blk, lat_k(0), tk)
    else:
        m_blk = score(lat_k(0)(), 0, tk)

    def middle(i, m_carry):
        return key_block(lat_k(i), lat_v(i), tk, m_carry, lat_k(i + 1), tk)

    m_blk = lax.fori_loop(0, nb - 1, middle, m_blk)
    key_block(lat_k(nb - 1), lat_v(nb - 1), tk, m_blk, None, 0)

    outs = []
    for g in range(GROUP):
        acc = acc_ref[g]
        outs.append(acc[0:HEAD_DIM, :] * (1.0 / acc[HEAD_DIM:HEAD_DIM + 1, :]))
    o_ref[0] = jnp.concatenate(outs, axis=0).T.astype(BF16)


def _attention(q, krep, vt, kc_rep=None, vc_t=None):
    b, n, _ = q.shape
    tk = vt.shape[-1]
    tq = TOKEN_TILE
    p_len = 0 if kc_rep is None else kc_rep.shape[2]
    ins, specs = [q], [pl.BlockSpec((1, tq, KV_W), lambda bi, h, i: (bi, i, h))]
    if p_len:
        ins += [kc_rep, vc_t]
        specs += [pl.BlockSpec((1, 1, p_len, KV_W), lambda bi, h, i: (bi, h, 0, 0)),
                  pl.BlockSpec((1, HEAD_DIM, p_len), lambda bi, h, i: (bi, h, 0))]
    ins += [krep, vt]
    specs += [pl.BlockSpec((1, 1, n, KV_W), lambda bi, h, i: (bi, h, 0, 0)),
              pl.BlockSpec((1, n // tk, HEAD_DIM, tk), lambda bi, h, i: (bi, 0, h, 0))]
    return pl.pallas_call(
        functools.partial(_attn_body, p_len=p_len, n=n, tq=tq, tk=tk),
        out_shape=jax.ShapeDtypeStruct((b, n, Q_W), BF16),
        grid=(b, N_KV_HEADS, n // tq),
        in_specs=specs,
        out_specs=pl.BlockSpec((1, tq, KV_W), lambda bi, h, i: (bi, i, h)),
        scratch_shapes=[pltpu.VMEM((GROUP, KV_W, tq), BF16),
                        pltpu.VMEM((GROUP, max(tk, p_len), tq), F32),
                        pltpu.VMEM((GROUP, 1, tq), F32),
                        pltpu.VMEM((GROUP, HEAD_DIM + ONES_ROWS, tq), F32)],
        compiler_params=_cparams(("arbitrary", "arbitrary", "arbitrary")),
        name="attention_lat" if p_len else "attention_ctx",
    )(*ins)


def _split_bf16(x):
    hi = x.astype(BF16)
    lo = (x - hi.astype(F32)).astype(BF16)
    return hi, lo


def _postmix_body(attn_ref, u_ref, up_ref, un_ref, gb_ref, sga_ref, sgat_ref, x_ref, mod_ref, n2_ref, cw_ref,
                  wa_ref, wc_ref, wo_ref, wr_ref, x1_ref, h2p_ref, affn_ref, afft_ref, *, tm):
    d = D_MODEL
    i = pl.program_id(1)
    last = pl.num_programs(1) - 1
    u = u_ref[0]
    row = lax.broadcasted_iota(I32, (tm, CONV_W), 0)
    prev_row = jnp.where(i > 0, up_ref[0, SUBLANES - 1:SUBLANES, :], 0.0)
    next_row = jnp.where(i < last, un_ref[0, 0:1, :], 0.0)
    u_m1 = jnp.where(row == 0, prev_row, pltpu.roll(u, 1, 0))
    u_p1 = jnp.where(row == tm - 1, next_row, pltpu.roll(u, tm - 1, 0))
    conv = cw_ref[0:1, :] * u_m1 + cw_ref[1:2, :] * u + cw_ref[2:3, :] * u_p1
    cv = (gb_ref[0].astype(F32) * conv).astype(BF16)
    conv_o = _mm(cv, wc_ref[...])
    attn_o = _mm(attn_ref[0], wa_ref[...])
    merged = sga_ref[0].astype(F32) * conv_o + sgat_ref[0].astype(F32) * attn_o
    mix = _mm(merged.astype(BF16), wo_ref[...])
    x1 = x_ref[0] + mod_ref[0, :, 2 * d:3 * d] * mix
    x1_ref[0] = x1
    h2 = (_rms(x1) * n2_ref[...]) * (1.0 + mod_ref[0, :, 4 * d:5 * d]) + mod_ref[0, :, 3 * d:4 * d]

    h_hi, h_lo = _split_bf16(h2)
    half = d // 2
    w_lo = lax.bitcast_convert_type(h_hi[:, :half].astype(F32), U32) >> 16
    w_hi = lax.bitcast_convert_type(h_hi[:, half:].astype(F32), U32) & jnp.uint32(0xFFFF0000)
    h2p_ref[0] = w_lo | w_hi

    r_hi, r_lo = _split_bf16(wr_ref[...])
    logits = (_mm(h_hi, r_hi) + _mm(h_hi, r_lo)
              + _mm(h_lo, r_hi))
    lane = lax.broadcasted_iota(I32, (tm, LANES), 1)
    valid = lane < N_EXPERTS
    logits = jnp.where(valid, logits, -jnp.inf)
    e = jnp.exp(logits - jnp.max(logits, axis=-1, keepdims=True))
    e = jnp.where(valid, e, 0.0)
    aff = e / jnp.sum(e, axis=-1, keepdims=True)
    affn_ref[0] = aff
    afft_ref[0] = aff.T[0:N_EXPERTS, :]


def _postmix(attn, u, gb, sga, sgat, x, mod_l, row_of_batch, lw):
    b, n, d = x.shape
    tm = TOKEN_TILE
    nsub = n // SUBLANES
    per = tm // SUBLANES
    tok_spec = lambda w: pl.BlockSpec((1, tm, w), lambda bi, i: (bi, i, 0))
    specs = [tok_spec(Q_W), tok_spec(CONV_W),
             pl.BlockSpec((1, SUBLANES, CONV_W), lambda bi, i: (bi, jnp.maximum(i * per - 1, 0), 0)),
             pl.BlockSpec((1, SUBLANES, CONV_W), lambda bi, i: (bi, jnp.minimum((i + 1) * per, nsub - 1), 0)),
             tok_spec(CONV_W), tok_spec(d), tok_spec(d), tok_spec(d),
             pl.BlockSpec((1, 1, 6 * d), lambda bi, i: (row_of_batch(bi), 0, 0)),
             _const_spec((1, d)), _const_spec((SUBLANES, CONV_W)),
             _const_spec((Q_W, d)), _const_spec((CONV_W, d)), _const_spec((d, d)), _const_spec((d, LANES))]
    outs = [jax.ShapeDtypeStruct((b, n, d), F32), jax.ShapeDtypeStruct((b, n, d // 2), U32),
            jax.ShapeDtypeStruct((b, n, LANES), F32), jax.ShapeDtypeStruct((b, N_EXPERTS, n), F32)]
    ospecs = [tok_spec(d), tok_spec(d // 2), tok_spec(LANES),
              pl.BlockSpec((1, N_EXPERTS, tm), lambda bi, i: (bi, 0, i))]
    return pl.pallas_call(
        functools.partial(_postmix_body, tm=tm),
        out_shape=outs, grid=(b, n // tm), in_specs=specs, out_specs=ospecs,
        compiler_params=_cparams(("arbitrary", "arbitrary")),
        name="postmix",
    )(attn, u, u, u, gb, sga, sgat, x, mod_l, lw["norm2"], lw["conv_w"], lw["w_attn_out"], lw["w_conv_out"],
      lw["w_o"], lw["w_router"])


def _topk_body(aff_ref, ut_ref, idx_ref, sel_ref, pos_ref, *, gb, n, cap):
    ne = N_EXPERTS
    rows_all = gb * ne
    a = aff_ref[...].reshape(rows_all, n)
    bits = lax.bitcast_convert_type(a, I32)

    def count(mask):
        return jnp.sum(jnp.where(mask, 1.0, 0.0), axis=1, keepdims=True)

    def thr_step(t, thr):
        cand = thr | (jnp.int32(1) << (30 - t))
        return jnp.where(count(bits >= cand) >= cap, cand, thr)

    thr = lax.fori_loop(0, 31, thr_step, jnp.zeros((rows_all, 1), I32))
    gt = bits > thr
    eq = bits == thr
    need = cap - count(gt)
    tok = lax.broadcasted_iota(I32, (rows_all, n), 1)
    nbits = int(np.log2(n)) + 1

    def tie_step(t, bound):
        cand = bound + (jnp.int32(1) << (nbits - 1 - t))
        ok = (cand <= n) & (count(eq & (tok < cand)) <= need)
        return jnp.where(ok, cand, bound)

    bound = lax.fori_loop(0, nbits, tie_step, jnp.zeros((rows_all, 1), I32))
    sel_ref[...] = jnp.where(gt | (eq & (tok < bound)), 1.0, 0.0)

    cb = min(CUMSUM_BLOCK, n)
    nblk = n // cb
    nrb = max(cap // RANK_BLOCK, 1)
    width = min(cap, RANK_BLOCK)
    tiles = RANK_BLOCK // SUBLANES
    lane = lax.broadcasted_iota(I32, (RANK_BLOCK, LANES), 1)
    rank0 = (lax.broadcasted_iota(I32, (tiles, SUBLANES, LANES), 0) * SUBLANES
             + lax.broadcasted_iota(I32, (tiles, SUBLANES, LANES), 1)).astype(F32)

    def batch_step(bi, carry):
        sel = sel_ref[pl.ds(pl.multiple_of(bi * ne, ne), ne), :]
        off = jnp.zeros((ne, 1), F32)
        for k in range(nblk):
            blk = _mm(sel[:, k * cb:(k + 1) * cb].astype(BF16), ut_ref[...])
            pos = blk + off
            for ex in range(ne):
                pos_ref[ex, :, k * cb:(k + 1) * cb] = jnp.broadcast_to(pos[ex:ex + 1, :], (SUBLANES, cb))
            off = off + blk[:, cb - 1:cb]

        def expert_step(ei, cols):
            for rb in range(nrb):
                rank = rank0 + float(rb * RANK_BLOCK)
                cnt = jnp.zeros((tiles, SUBLANES, LANES), F32)
                for lc in range(n // LANES):
                    prow = pos_ref[ei, :, lc * LANES:(lc + 1) * LANES]
                    cnt = cnt + jnp.where(prow[None] <= rank, 1.0, 0.0)
                col = jnp.sum(cnt.reshape(RANK_BLOCK, LANES), axis=1, keepdims=True)
                cols = jnp.where(lane == ei * nrb + rb, col, cols)
            return cols

        cols = lax.fori_loop(0, ne, expert_step, jnp.zeros((RANK_BLOCK, LANES), F32))
        rows = cols.T
        idx_ref[bi] = rows[0:ne * nrb, 0:width].astype(I32)
        return carry

    lax.fori_loop(0, gb, batch_step, 0)


def _topk(afft, consts):
    b, ne, n = afft.shape
    cap = max(1, EC_FACTOR * n // N_EXPERTS)
    nrb = max(cap // RANK_BLOCK, 1)
    width = min(cap, RANK_BLOCK)
    cb = min(CUMSUM_BLOCK, n)
    gb = min(b, max(1, TOPK_ROWS // (ne * n)))
    assert b % gb == 0
    idx = pl.pallas_call(
        functools.partial(_topk_body, gb=gb, n=n, cap=cap),
        out_shape=jax.ShapeDtypeStruct((b, ne * nrb, width), I32),
        grid=(b // gb,),
        in_specs=[pl.BlockSpec((gb, ne, n), lambda bi: (bi, 0, 0)), _const_spec((cb, cb))],
        out_specs=pl.BlockSpec((gb, ne * nrb, width), lambda bi: (bi, 0, 0)),
        scratch_shapes=[pltpu.VMEM((gb * ne, n), F32), pltpu.VMEM((ne, SUBLANES, n), F32)],
        compiler_params=_cparams(("arbitrary",)),
        name="topk",
    )(afft, consts["ut"][:cb, :cb])
    return idx.reshape(b, ne, cap)


def _ffn_body(idx_ref, h2p_ref, affn_ref, wg_ref, wu_ref, wd_ref, y_ref, xs_ref, ws_ref, *, rows, e_axis):
    def gather(j, carry):
        for k in range(ROW_UNROLL):
            r = j * ROW_UNROLL + k
            t = idx_ref[0, 0, r]
            xs_ref[pl.ds(r, 1), :] = h2p_ref[0, pl.ds(t, 1), :]
            ws_ref[pl.ds(r, 1), :] = affn_ref[0, pl.ds(t, 1), :]
        return carry

    lax.fori_loop(0, rows // ROW_UNROLL, gather, 0)
    words = xs_ref[...]
    x_lo = lax.bitcast_convert_type(words << 16, F32).astype(BF16)
    x_hi = lax.bitcast_convert_type(words & jnp.uint32(0xFFFF0000), F32).astype(BF16)
    half = D_MODEL // 2

    def up_proj(w_ref):
        return (_mm(x_lo, w_ref[0, 0:half, :])
                + _mm(x_hi, w_ref[0, half:, :]))

    gate = up_proj(wg_ref)
    hidden = ((gate * jax.nn.sigmoid(gate)) * up_proj(wu_ref)).astype(BF16)
    y = _mm(hidden, wd_ref[0])
    lane = lax.broadcasted_iota(I32, (rows, LANES), 1)
    val = jnp.sum(jnp.where(lane == pl.program_id(e_axis), ws_ref[...], 0.0), axis=1, keepdims=True)
    y_ref[0, 0] = y * val


def _moe_ffn(idx_steps, h2p, affn, lw, group, expert_outer):
    ng, ne, rows = idx_steps.shape
    gtok = h2p.shape[1] * group
    h2p = h2p.reshape(ng, gtok, D_MODEL // 2)
    affn = affn.reshape(ng, gtok, LANES)
    if expert_outer:
        grid = (ne, ng)
        ge = lambda a, b_: (b_, a)
    else:
        grid = (ng, ne)
        ge = lambda a, b_: (a, b_)
    g_of = lambda a, b_: ge(a, b_)[0]
    e_of = lambda a, b_: ge(a, b_)[1]
    idx_flat = idx_steps.reshape(ng * ne, 1, rows)
    w_spec = lambda r, c: pl.BlockSpec((1, r, c), lambda a, b_: (e_of(a, b_), 0, 0))
    return pl.pallas_call(
        functools.partial(_ffn_body, rows=rows, e_axis=0 if expert_outer else 1),
        out_shape=jax.ShapeDtypeStruct((ng, ne, rows, D_MODEL), F32),
        grid=grid,
        in_specs=[pl.BlockSpec((1, 1, rows), lambda a, b_: (g_of(a, b_) * ne + e_of(a, b_), 0, 0),
                               memory_space=pltpu.SMEM),
                  pl.BlockSpec((1, gtok, D_MODEL // 2), lambda a, b_: (g_of(a, b_), 0, 0)),
                  pl.BlockSpec((1, gtok, LANES), lambda a, b_: (g_of(a, b_), 0, 0)),
                  w_spec(D_MODEL, D_EXPERT), w_spec(D_MODEL, D_EXPERT), w_spec(D_EXPERT, D_MODEL)],
        out_specs=pl.BlockSpec((1, 1, rows, D_MODEL), lambda a, b_: (g_of(a, b_), e_of(a, b_), 0, 0)),
        scratch_shapes=[pltpu.VMEM((rows, D_MODEL // 2), U32), pltpu.VMEM((rows, LANES), F32)],
        compiler_params=_cparams(("arbitrary", "arbitrary")),
        name="moe_ffn",
    )(idx_flat, h2p, affn, lw["w_gate"], lw["w_up"], lw["w_down"])


def _combine_body(idx_ref, y_ref, o_ref, *, rows):
    @pl.when(pl.program_id(1) == 0)
    def _zero():
        o_ref[...] = jnp.zeros(o_ref.shape, F32)

    def scatter(j, carry):
        toks = [idx_ref[0, 0, j * ROW_UNROLL + k] for k in range(ROW_UNROLL)]
        vals = [o_ref[0, pl.ds(toks[k], 1), :] + y_ref[0, 0, pl.ds(j * ROW_UNROLL + k, 1), :]
                for k in range(ROW_UNROLL)]
        for k in range(ROW_UNROLL):
            o_ref[0, pl.ds(toks[k], 1), :] = vals[k]
        return carry

    lax.fori_loop(0, rows // ROW_UNROLL, scatter, 0)


def _combine(idx_steps, ysel, gtok):
    ng, ne, rows = idx_steps.shape
    return pl.pallas_call(
        functools.partial(_combine_body, rows=rows),
        out_shape=jax.ShapeDtypeStruct((ng, gtok, D_MODEL), F32),
        grid=(ng, ne),
        in_specs=[pl.BlockSpec((1, 1, rows), lambda g, e: (g * ne + e, 0, 0), memory_space=pltpu.SMEM),
                  pl.BlockSpec((1, 1, rows, D_MODEL), lambda g, e: (g, e, 0, 0))],
        out_specs=pl.BlockSpec((1, gtok, D_MODEL), lambda g, e: (g, 0, 0)),
        compiler_params=_cparams(("arbitrary", "arbitrary")),
        name="moe_combine",
    )(idx_steps.reshape(ng * ne, 1, rows), ysel)


def _expert_choice(h2p, affn, afft, lw, consts, group, expert_outer):
    b, n, _ = h2p.shape
    idx = _topk(afft, consts)
    cap = idx.shape[-1]
    ng = b // group
    offs = (jnp.arange(b, dtype=I32) % group * n).reshape(ng, group, 1, 1)
    idx_steps = (idx.reshape(ng, group, N_EXPERTS, cap) + offs).transpose(0, 2, 1, 3).reshape(ng, N_EXPERTS, group * cap)
    ysel = _moe_ffn(idx_steps, h2p, affn, lw, group, expert_outer)
    moe = _combine(idx_steps, ysel, group * n)
    return moe.reshape(b, n, D_MODEL)


def _final_body(x1_ref, moe_ref, mod_ref, g_ref, o_ref):
    d = D_MODEL
    x = x1_ref[0] + mod_ref[0, :, 5 * d:6 * d] * moe_ref[0]
    o_ref[0] = _rms(x) * g_ref[...]


def _final_norm(x1, moe, mod_l, row_of_batch, g):
    b, n, d = x1.shape
    tm = TOKEN_TILE
    tok = pl.BlockSpec((1, tm, d), lambda bi, i: (bi, i, 0))
    return pl.pallas_call(
        _final_body,
        out_shape=jax.ShapeDtypeStruct((b, n, d), F32),
        grid=(b, n // tm),
        in_specs=[tok, tok, pl.BlockSpec((1, 1, 6 * d), lambda bi, i: (row_of_batch(bi), 0, 0)), _const_spec((1, d))],
        out_specs=tok,
        compiler_params=_cparams(("arbitrary", "arbitrary")),
        name="final_norm",
    )(x1, moe, mod_l, g)


def _rope_tables(n_lat):
    half = AXIS_DIM // 2
    inv_freq = ROPE_THETA ** (-jnp.arange(half, dtype=F32) / half)
    t = jnp.arange(n_lat, dtype=jnp.int32)
    row = (t // GRID_W).astype(F32)
    col = (t % GRID_W).astype(F32)
    lane = np.arange(LANES)
    dim = lane % HEAD_DIM
    use_col = jnp.asarray(dim >= AXIS_DIM)
    freq = inv_freq[jnp.asarray(dim % half)]
    pos = jnp.where(use_col[None, :], col[:, None], row[:, None])
    ang = pos * freq[None, :]
    sign = jnp.asarray(np.where(dim % AXIS_DIM < half, -1.0, 1.0), dtype=F32)
    return jnp.cos(ang), jnp.sin(ang) * sign[None, :]


def _block_diag_ones(width):
    seg = np.arange(width) // HEAD_DIM
    return jnp.asarray(seg[:, None] == seg[None, :], dtype=BF16)


def kernel(x_prompt, x_sample, cache_k, cache_v, c, c_ctx, w_mod, b_mod, norm1, norm2, w_in, q_norm, k_norm,
           conv_w, w_conv_out, w_attn_out, w_o, w_router, w_gate, w_up, w_down, final_norm):
    depth = w_mod.shape[0]
    bc, nc, d = x_prompt.shape
    bl, nl, _ = x_sample.shape
    p_len = cache_k.shape[2]
    assert d == D_MODEL and bl + 1 <= MOD_ROWS
    assert nc % TOKEN_TILE == 0 and nl % KEY_BLOCK == 0 and p_len % TOKEN_TILE == 0

    cmat = jnp.zeros((MOD_ROWS, d), F32).at[0].set(c_ctx).at[1:1 + bl].set(c)
    mods = _modulation(cmat, w_mod, b_mod).reshape(depth, MOD_ROWS, 1, 6 * d)

    cos_t, sin_t = _rope_tables(nl)
    ut = np.arange(CUMSUM_BLOCK)
    consts = {"bdq": _block_diag_ones(Q_W), "bdk": _block_diag_ones(KV_W), "cos": cos_t, "sin": sin_t,
              "ut": jnp.asarray(ut[:, None] <= ut[None, :], dtype=BF16)}

    kc_rep = jnp.tile(cache_k.transpose(0, 1, 3, 2, 4), (1, 1, 1, 1, GROUP)).astype(BF16)
    vc_t = cache_v.transpose(0, 1, 3, 4, 2).reshape(bl, depth, KV_W, p_len).astype(BF16)

    ctx_row = lambda bi: 0
    lat_row = lambda bi: bi + 1
    ctx_group = min(8, bc)

    state = {"ctx": (x_prompt, None), "lat": (x_sample, None)}
    new_k, new_v = [], []
    for l in range(depth):
        lw = {
            "norm1": norm1[l].reshape(1, d), "norm2": norm2[l].reshape(1, d),
            "w_in": w_in[l].astype(BF16),
            "gq": jnp.tile(q_norm[l], N_HEADS).reshape(1, Q_W), "gk": jnp.tile(k_norm[l], N_KV_HEADS).reshape(1, KV_W),
            "conv_w": jnp.zeros((SUBLANES, CONV_W), F32).at[0:3].set(conv_w[l]),
            "w_conv_out": w_conv_out[l].astype(BF16), "w_attn_out": w_attn_out[l].astype(BF16),
            "w_o": w_o[l].astype(BF16),
            "w_router": jnp.zeros((d, LANES), F32).at[:, 0:N_EXPERTS].set(w_router[l]),
            "w_gate": w_gate[l].astype(BF16), "w_up": w_up[l].astype(BF16), "w_down": w_down[l].astype(BF16),
        }
        mod_l = mods[l]
        modp = mods[l - 1] if l else None
        for name in ("ctx", "lat"):
            is_lat = name == "lat"
            row_fn = lat_row if is_lat else ctx_row
            xa, xb = state[name]
            res = _premix(xa, xb, modp if xb is not None else None, mod_l, row_fn, lw, consts, is_lat)
            q, krep, vt = res[0:3]
            pos = 3
            if not is_lat:
                new_k.append(res[3])
                new_v.append(res[4])
                pos = 5
            u, gb, sga, sgat = res[pos:pos + 4]
            x_cur = res[pos + 4] if xb is not None else xa
            if is_lat:
                attn = _attention(q, krep, vt, kc_rep[:, l], vc_t[:, l])
            else:
                attn = _attention(q, krep, vt)
            x1, h2p, affn, afft = _postmix(attn, u, gb, sga, sgat, x_cur, mod_l, row_fn, lw)
            moe = _expert_choice(h2p, affn, afft, lw, consts, group=1 if is_lat else ctx_group,
                                 expert_outer=not is_lat)
            state[name] = (x1, moe)

    g = final_norm.reshape(1, d)
    y_prompt = _final_norm(*state["ctx"], mods[depth - 1], ctx_row, g)
    y_sample = _final_norm(*state["lat"], mods[depth - 1], lat_row, g)
    shape_kv = (bc, depth, nc, N_KV_HEADS, HEAD_DIM)
    new_cache_k = jnp.stack(new_k, axis=1).reshape(shape_kv)
    new_cache_v = jnp.stack(new_v, axis=1).reshape(shape_kv)
    return (y_prompt, y_sample, new_cache_k, new_cache_v)
```

```python
import functools

import jax
import jax.numpy as jnp
import numpy as np
from jax import lax
from jax.experimental import pallas as pl
from jax.experimental.pallas import tpu as pltpu

F32 = jnp.float32
BF16 = jnp.bfloat16
I32 = jnp.int32
U32 = jnp.uint32

D_MODEL = 1024
N_HEADS = 16
N_KV_HEADS = 4
HEAD_DIM = 64
GROUP = N_HEADS // N_KV_HEADS
Q_W = N_HEADS * HEAD_DIM
KV_W = N_KV_HEADS * HEAD_DIM
CONV_W = D_MODEL // 2
IN_W = Q_W + 2 * KV_W + 3 * CONV_W + 2 * D_MODEL
N_EXPERTS = 16
EC_FACTOR = 2
D_EXPERT = 1024
GRID_W = 64
AXIS_DIM = HEAD_DIM // 2
ROPE_THETA = 10000.0
EPS = 1e-6
MOD_ROWS = 16

LANES = 128
SUBLANES = 8
VMEM_LIMIT_BYTES = 56 * 1024 * 1024

TOKEN_TILE = 256
KEY_BLOCK = 1024
QUERY_TILE = 512
CUMSUM_BLOCK = 256
RANK_BLOCK = 128
TOPK_ROWS = 512 * 1024
ROW_UNROLL = 8
ONES_ROWS = 16
LOG2_E = 1.4426950408889634


def _cparams(sem):
    return pltpu.CompilerParams(dimension_semantics=sem, vmem_limit_bytes=VMEM_LIMIT_BYTES)


def _mm(a, b):
    return jnp.dot(a, b, preferred_element_type=F32)


def _const_spec(shape):
    nd = len(shape)
    return pl.BlockSpec(shape, lambda *_: (0,) * nd)


def _mod_body(c_ref, w_ref, b_ref, o_ref):
    c = c_ref[...]
    s = c * jax.nn.sigmoid(c)
    o_ref[0] = _mm(s.astype(BF16), w_ref[0].astype(BF16)) + b_ref[0]


def _modulation(cmat, w_mod, b_mod):
    depth = w_mod.shape[0]
    tn = 1536
    return pl.pallas_call(
        _mod_body,
        out_shape=jax.ShapeDtypeStruct((depth, MOD_ROWS, 6 * D_MODEL), F32),
        grid=(depth, 6 * D_MODEL // tn),
        in_specs=[
            pl.BlockSpec((MOD_ROWS, D_MODEL), lambda l, j: (0, 0)),
            pl.BlockSpec((1, D_MODEL, tn), lambda l, j: (l, 0, j)),
            pl.BlockSpec((1, 1, tn), lambda l, j: (l, 0, j)),
        ],
        out_specs=pl.BlockSpec((1, MOD_ROWS, tn), lambda l, j: (l, 0, j)),
        compiler_params=_cparams(("arbitrary", "arbitrary")),
        name="modulation",
    )(cmat, w_mod, b_mod.reshape(depth, 1, 6 * D_MODEL))


def _rms(x):
    return x * lax.rsqrt(jnp.mean(x * x, axis=-1, keepdims=True) + EPS)


def _premix_body(*refs, is_lat, has_prev, tm):
    it = iter(refs)
    xa_ref = next(it)
    if has_prev:
        xb_ref = next(it)
        modp_ref = next(it)
    mod_ref = next(it)
    n1_ref = next(it)
    win_ref = next(it)
    gq_ref = next(it)
    gk_ref = next(it)
    bdq_ref = next(it)
    bdk_ref = next(it)
    if is_lat:
        cos_ref = next(it)
        sin_ref = next(it)
    q_ref = next(it)
    krep_ref = next(it)
    vt_ref = next(it)
    if not is_lat:
        kc_ref = next(it)
        vc_ref = next(it)
    u_ref = next(it)
    gb_ref = next(it)
    sga_ref = next(it)
    sgat_ref = next(it)
    if has_prev:
        xn_ref = next(it)

    d = D_MODEL
    x = xa_ref[0]
    if has_prev:
        x = x + modp_ref[0, :, 5 * d:6 * d] * xb_ref[0]
        xn_ref[0] = x
    sh1 = mod_ref[0, :, 0:d]
    sc1 = mod_ref[0, :, d:2 * d]
    h = (_rms(x) * n1_ref[...]) * (1.0 + sc1) + sh1
    hb = h.astype(BF16)

    def proj(lo, hi):
        return _mm(hb, win_ref[:, lo:hi])

    lane = lax.broadcasted_iota(I32, (tm, LANES), 1)
    first_half = (lane % AXIS_DIM) < (AXIS_DIM // 2)
    low_head = lane < HEAD_DIM

    def rope(chunk, c):
        if not is_lat:
            return chunk
        del c
        partner = jnp.where(first_half, pltpu.roll(chunk, LANES - AXIS_DIM // 2, 1),
                            pltpu.roll(chunk, AXIS_DIM // 2, 1))
        return chunk * cos_ref[...] + partner * sin_ref[...]

    pq = proj(0, Q_W)
    ssq = _mm((pq * pq).astype(BF16), bdq_ref[...])
    qn = pq * lax.rsqrt(ssq * (1.0 / HEAD_DIM) + EPS) * gq_ref[...]
    for c in range(Q_W // LANES):
        qc = rope(qn[:, c * LANES:(c + 1) * LANES], c)
        q_ref[0, :, c * LANES:(c + 1) * LANES] = (qc * (HEAD_DIM ** -0.5 * LOG2_E)).astype(BF16)

    pk = proj(Q_W, Q_W + KV_W)
    ssk = _mm((pk * pk).astype(BF16), bdk_ref[...])
    kn = pk * lax.rsqrt(ssk * (1.0 / HEAD_DIM) + EPS) * gk_ref[...]
    if not is_lat:
        kc_ref[0] = kn
    for c in range(KV_W // LANES):
        kc = rope(kn[:, c * LANES:(c + 1) * LANES], c)
        rolled = pltpu.roll(kc, HEAD_DIM, 1)
        even = jnp.where(low_head, kc, rolled).astype(BF16)
        odd = jnp.where(low_head, rolled, kc).astype(BF16)
        for s in range(KV_W // LANES):
            krep_ref[0, 2 * c, :, s * LANES:(s + 1) * LANES] = even
            krep_ref[0, 2 * c + 1, :, s * LANES:(s + 1) * LANES] = odd

    pv = proj(Q_W + KV_W, Q_W + 2 * KV_W)
    if not is_lat:
        vc_ref[0] = pv
    vt_ref[0, 0] = pv.T.astype(BF16)

    o = Q_W + 2 * KV_W
    gb_ref[0] = proj(o, o + CONV_W).astype(BF16)
    u_ref[0] = proj(o + CONV_W, o + 2 * CONV_W) * proj(o + 2 * CONV_W, o + 3 * CONV_W)
    o = o + 3 * CONV_W
    sga_ref[0] = jax.nn.sigmoid(proj(o, o + d)).astype(BF16)
    sgat_ref[0] = jax.nn.sigmoid(proj(o + d, o + 2 * d)).astype(BF16)


def _premix(xa, xb, modp, mod_l, row_of_batch, lw, consts, is_lat):
    b, n, d = xa.shape
    tm = TOKEN_TILE
    tk = min(KEY_BLOCK, n)
    has_prev = xb is not None
    sub = tk // tm
    tok_spec = lambda w: pl.BlockSpec((1, tm, w), lambda bi, i: (bi, i, 0))
    mod_spec = pl.BlockSpec((1, 1, 6 * d), lambda bi, i: (row_of_batch(bi), 0, 0))

    ins, specs = [xa], [tok_spec(d)]
    if has_prev:
        ins += [xb, modp]
        specs += [tok_spec(d), mod_spec]
    ins += [mod_l, lw["norm1"], lw["w_in"], lw["gq"], lw["gk"], consts["bdq"], consts["bdk"]]
    specs += [mod_spec, _const_spec((1, d)), _const_spec((d, IN_W)), _const_spec((1, Q_W)),
              _const_spec((1, KV_W)), _const_spec((Q_W, Q_W)), _const_spec((KV_W, KV_W))]
    if is_lat:
        ins += [consts["cos"], consts["sin"]]
        specs += [pl.BlockSpec((tm, LANES), lambda bi, i: (i, 0))] * 2

    outs = [jax.ShapeDtypeStruct((b, n, Q_W), BF16),
            jax.ShapeDtypeStruct((b, N_KV_HEADS, n, KV_W), BF16),
            jax.ShapeDtypeStruct((b, n // tk, KV_W, tk), BF16)]
    ospecs = [tok_spec(Q_W),
              pl.BlockSpec((1, N_KV_HEADS, tm, KV_W), lambda bi, i: (bi, 0, i, 0)),
              pl.BlockSpec((1, 1, KV_W, tm), lambda bi, i: (bi, i // sub, 0, i % sub))]
    if not is_lat:
        outs += [jax.ShapeDtypeStruct((b, n, KV_W), F32)] * 2
        ospecs += [tok_spec(KV_W)] * 2
    outs += [jax.ShapeDtypeStruct((b, n, CONV_W), F32), jax.ShapeDtypeStruct((b, n, CONV_W), BF16),
             jax.ShapeDtypeStruct((b, n, d), BF16), jax.ShapeDtypeStruct((b, n, d), BF16)]
    ospecs += [tok_spec(CONV_W), tok_spec(CONV_W), tok_spec(d), tok_spec(d)]
    if has_prev:
        outs.append(jax.ShapeDtypeStruct((b, n, d), F32))
        ospecs.append(tok_spec(d))

    res = pl.pallas_call(
        functools.partial(_premix_body, is_lat=is_lat, has_prev=has_prev, tm=tm),
        out_shape=outs, grid=(b, n // tm), in_specs=specs, out_specs=ospecs,
        compiler_params=_cparams(("arbitrary", "arbitrary")),
        name="premix_lat" if is_lat else "premix_ctx",
    )(*ins)
    return list(res)


def _attn_body(*refs, p_len, n, tq, tk):
    it = iter(refs)
    q_ref = next(it)
    if p_len:
        kc_ref = next(it)
        vc_ref = next(it)
    k_ref = next(it)
    v_ref = next(it)
    o_ref = next(it)
    qbd_ref = next(it)
    s_ref = next(it)
    m_ref = next(it)
    acc_ref = next(it)

    qt = q_ref[0].astype(F32).T
    head_of_row = lax.broadcasted_iota(I32, (KV_W, tq), 0) // HEAD_DIM
    for g in range(GROUP):
        qbd_ref[g] = jnp.where(head_of_row == g, qt, 0.0).astype(BF16)
    m_ref[...] = jnp.full(m_ref.shape, -jnp.inf, F32)
    acc_ref[...] = jnp.zeros(acc_ref.shape, F32)

    def score(k_rows, g, size):
        s = _mm(k_rows, qbd_ref[g])
        s_ref[g, 0:size, :] = s
        return jnp.max(s, axis=0, keepdims=True)

    def softmax_pv(v_blk, g, size, m_blk):
        m_prev = m_ref[g]
        m_new = jnp.maximum(m_prev, m_blk)
        alpha = jnp.exp2(m_prev - m_new)
        p = jnp.exp2(s_ref[g, 0:size, :] - m_new).astype(BF16)
        v_ext = jnp.concatenate([v_blk, jnp.ones((ONES_ROWS, size), BF16)], axis=0)
        acc_ref[g] = alpha * acc_ref[g] + _mm(v_ext, p)
        m_ref[g] = m_new

    def key_block(get_k, get_v, size, m_blk, next_k, next_size):
        for g in range(GROUP):
            if g + 1 < GROUP:
                m_next = score(get_k(), g + 1, size)
            elif next_k is not None:
                m_next = score(next_k(), 0, next_size)
            else:
                m_next = None
            softmax_pv(get_v(), g, size, m_blk)
            m_blk = m_next
        return m_blk

    def lat_k(i):
        if isinstance(i, int):
            return lambda: k_ref[0, 0, i * tk:(i + 1) * tk, :]
        return lambda: k_ref[0, 0, pl.ds(pl.multiple_of(i * tk, tk), tk), :]

    def lat_v(i):
        return lambda: v_ref[0, i]

    nb = n // tk
    if p_len:
        cache_k = lambda: kc_ref[0, 0]
        m_blk = score(cache_k(), 0, p_len)
        m_blk = key_block(cache_k, lambda: vc_ref[0], p_len, m_blk, lat_k(0), tk)
    else:
        m_blk = score(lat_k(0)(), 0, tk)

    def middle(i, m_carry):
        return key_block(lat_k(i), lat_v(i), tk, m_carry, lat_k(i + 1), tk)

    m_blk = lax.fori_loop(0, nb - 1, middle, m_blk)
    key_block(lat_k(nb - 1), lat_v(nb - 1), tk, m_blk, None, 0)

    outs = []
    for g in range(GROUP):
        acc = acc_ref[g]
        outs.append(acc[0:HEAD_DIM, :] * (1.0 / acc[HEAD_DIM:HEAD_DIM + 1, :]))
    o_ref[0] = jnp.concatenate(outs, axis=0).T.astype(BF16)


def _attention(q, krep, vt, kc_rep=None, vc_t=None):
    b, n, _ = q.shape
    tk = vt.shape[-1]
    tq = min(QUERY_TILE, n)
    p_len = 0 if kc_rep is None else kc_rep.shape[2]
    ins, specs = [q], [pl.BlockSpec((1, tq, KV_W), lambda bi, h, i: (bi, i, h))]
    if p_len:
        ins += [kc_rep, vc_t]
        specs += [pl.BlockSpec((1, 1, p_len, KV_W), lambda bi, h, i: (bi, h, 0, 0)),
                  pl.BlockSpec((1, HEAD_DIM, p_len), lambda bi, h, i: (bi, h, 0))]
    ins += [krep, vt]
    specs += [pl.BlockSpec((1, 1, n, KV_W), lambda bi, h, i: (bi, h, 0, 0)),
              pl.BlockSpec((1, n // tk, HEAD_DIM, tk), lambda bi, h, i: (bi, 0, h, 0))]
    return pl.pallas_call(
        functools.partial(_attn_body, p_len=p_len, n=n, tq=tq, tk=tk),
        out_shape=jax.ShapeDtypeStruct((b, n, Q_W), BF16),
        grid=(b, N_KV_HEADS, n // tq),
        in_specs=specs,
        out_specs=pl.BlockSpec((1, tq, KV_W), lambda bi, h, i: (bi, i, h)),
        scratch_shapes=[pltpu.VMEM((GROUP, KV_W, tq), BF16),
                        pltpu.VMEM((GROUP, max(tk, p_len), tq), F32),
                        pltpu.VMEM((GROUP, 1, tq), F32),
                        pltpu.VMEM((GROUP, HEAD_DIM + ONES_ROWS, tq), F32)],
        compiler_params=_cparams(("arbitrary", "arbitrary", "arbitrary")),
        name="attention_lat" if p_len else "attention_ctx",
    )(*ins)


def _split_bf16(x):
    hi = x.astype(BF16)
    lo = (x - hi.astype(F32)).astype(BF16)
    return hi, lo


def _postmix_body(attn_ref, u_ref, up_ref, un_ref, gb_ref, sga_ref, sgat_ref, x_ref, mod_ref, n2_ref, cw_ref,
                  wa_ref, wc_ref, wo_ref, wr_ref, x1_ref, h2p_ref, affn_ref, afft_ref, *, tm):
    d = D_MODEL
    i = pl.program_id(1)
    last = pl.num_programs(1) - 1
    u = u_ref[0]
    row = lax.broadcasted_iota(I32, (tm, CONV_W), 0)
    prev_row = jnp.where(i > 0, up_ref[0, SUBLANES - 1:SUBLANES, :], 0.0)
    next_row = jnp.where(i < last, un_ref[0, 0:1, :], 0.0)
    u_m1 = jnp.where(row == 0, prev_row, pltpu.roll(u, 1, 0))
    u_p1 = jnp.where(row == tm - 1, next_row, pltpu.roll(u, tm - 1, 0))
    conv = cw_ref[0:1, :] * u_m1 + cw_ref[1:2, :] * u + cw_ref[2:3, :] * u_p1
    cv = (gb_ref[0].astype(F32) * conv).astype(BF16)
    conv_o = _mm(cv, wc_ref[...])
    attn_o = _mm(attn_ref[0], wa_ref[...])
    merged = sga_ref[0].astype(F32) * conv_o + sgat_ref[0].astype(F32) * attn_o
    mix = _mm(merged.astype(BF16), wo_ref[...])
    x1 = x_ref[0] + mod_ref[0, :, 2 * d:3 * d] * mix
    x1_ref[0] = x1
    h2 = (_rms(x1) * n2_ref[...]) * (1.0 + mod_ref[0, :, 4 * d:5 * d]) + mod_ref[0, :, 3 * d:4 * d]

    h_hi, h_lo = _split_bf16(h2)
    half = d // 2
    w_lo = lax.bitcast_convert_type(h_hi[:, :half].astype(F32), U32) >> 16
    w_hi = lax.bitcast_convert_type(h_hi[:, half:].astype(F32), U32) & jnp.uint32(0xFFFF0000)
    h2p_ref[0] = w_lo | w_hi

    r_hi, r_lo = _split_bf16(wr_ref[...])
    logits = (_mm(h_hi, r_hi) + _mm(h_hi, r_lo)
              + _mm(h_lo, r_hi))
    lane = lax.broadcasted_iota(I32, (tm, LANES), 1)
    valid = lane < N_EXPERTS
    logits = jnp.where(valid, logits, -jnp.inf)
    e = jnp.exp(logits - jnp.max(logits, axis=-1, keepdims=True))
    e = jnp.where(valid, e, 0.0)
    aff = e / jnp.sum(e, axis=-1, keepdims=True)
    affn_ref[0] = aff
    afft_ref[0] = aff.T[0:N_EXPERTS, :]


def _postmix(attn, u, gb, sga, sgat, x, mod_l, row_of_batch, lw):
    b, n, d = x.shape
    tm = TOKEN_TILE
    nsub = n // SUBLANES
    per = tm // SUBLANES
    tok_spec = lambda w: pl.BlockSpec((1, tm, w), lambda bi, i: (bi, i, 0))
    specs = [tok_spec(Q_W), tok_spec(CONV_W),
             pl.BlockSpec((1, SUBLANES, CONV_W), lambda bi, i: (bi, jnp.maximum(i * per - 1, 0), 0)),
             pl.BlockSpec((1, SUBLANES, CONV_W), lambda bi, i: (bi, jnp.minimum((i + 1) * per, nsub - 1), 0)),
             tok_spec(CONV_W), tok_spec(d), tok_spec(d), tok_spec(d),
             pl.BlockSpec((1, 1, 6 * d), lambda bi, i: (row_of_batch(bi), 0, 0)),
             _const_spec((1, d)), _const_spec((SUBLANES, CONV_W)),
             _const_spec((Q_W, d)), _const_spec((CONV_W, d)), _const_spec((d, d)), _const_spec((d, LANES))]
    outs = [jax.ShapeDtypeStruct((b, n, d), F32), jax.ShapeDtypeStruct((b, n, d // 2), U32),
            jax.ShapeDtypeStruct((b, n, LANES), F32), jax.ShapeDtypeStruct((b, N_EXPERTS, n), F32)]
    ospecs = [tok_spec(d), tok_spec(d // 2), tok_spec(LANES),
              pl.BlockSpec((1, N_EXPERTS, tm), lambda bi, i: (bi, 0, i))]
    return pl.pallas_call(
        functools.partial(_postmix_body, tm=tm),
        out_shape=outs, grid=(b, n // tm), in_specs=specs, out_specs=ospecs,
        compiler_params=_cparams(("arbitrary", "arbitrary")),
        name="postmix",
    )(attn, u, u, u, gb, sga, sgat, x, mod_l, lw["norm2"], lw["conv_w"], lw["w_attn_out"], lw["w_conv_out"],
      lw["w_o"], lw["w_router"])


def _topk_body(aff_ref, ut_ref, idx_ref, sel_ref, pos_ref, *, gb, n, cap):
    ne = N_EXPERTS
    rows_all = gb * ne
    a = aff_ref[...].reshape(rows_all, n)
    bits = lax.bitcast_convert_type(a, I32)

    def count(mask):
        return jnp.sum(jnp.where(mask, 1.0, 0.0), axis=1, keepdims=True)

    def thr_step(t, thr):
        cand = thr | (jnp.int32(1) << (30 - t))
        return jnp.where(count(bits >= cand) >= cap, cand, thr)

    thr = lax.fori_loop(0, 31, thr_step, jnp.zeros((rows_all, 1), I32))
    gt = bits > thr
    eq = bits == thr
    need = cap - count(gt)
    tok = lax.broadcasted_iota(I32, (rows_all, n), 1)
    nbits = int(np.log2(n)) + 1

    def tie_step(t, bound):
        cand = bound + (jnp.int32(1) << (nbits - 1 - t))
        ok = (cand <= n) & (count(eq & (tok < cand)) <= need)
        return jnp.where(ok, cand, bound)

    bound = lax.fori_loop(0, nbits, tie_step, jnp.zeros((rows_all, 1), I32))
    sel_ref[...] = jnp.where(gt | (eq & (tok < bound)), 1.0, 0.0)

    cb = min(CUMSUM_BLOCK, n)
    nblk = n // cb
    nrb = max(cap // RANK_BLOCK, 1)
    width = min(cap, RANK_BLOCK)
    tiles = RANK_BLOCK // SUBLANES
    lane = lax.broadcasted_iota(I32, (RANK_BLOCK, LANES), 1)
    rank0 = (lax.broadcasted_iota(I32, (tiles, SUBLANES, LANES), 0) * SUBLANES
             + lax.broadcasted_iota(I32, (tiles, SUBLANES, LANES), 1)).astype(F32)

    def batch_step(bi, carry):
        sel = sel_ref[pl.ds(pl.multiple_of(bi * ne, ne), ne), :]
        off = jnp.zeros((ne, 1), F32)
        for k in range(nblk):
            blk = _mm(sel[:, k * cb:(k + 1) * cb].astype(BF16), ut_ref[...])
            pos = blk + off
            for ex in range(ne):
                pos_ref[ex, :, k * cb:(k + 1) * cb] = jnp.broadcast_to(pos[ex:ex + 1, :], (SUBLANES, cb))
            off = off + blk[:, cb - 1:cb]

        def expert_step(ei, cols):
            for rb in range(nrb):
                rank = rank0 + float(rb * RANK_BLOCK)
                cnt = jnp.zeros((tiles, SUBLANES, LANES), F32)
                for lc in range(n // LANES):
                    prow = pos_ref[ei, :, lc * LANES:(lc + 1) * LANES]
                    cnt = cnt + jnp.where(prow[None] <= rank, 1.0, 0.0)
                col = jnp.sum(cnt.reshape(RANK_BLOCK, LANES), axis=1, keepdims=True)
                cols = jnp.where(lane == ei * nrb + rb, col, cols)
            return cols

        cols = lax.fori_loop(0, ne, expert_step, jnp.zeros((RANK_BLOCK, LANES), F32))
        rows = cols.T
        idx_ref[bi] = rows[0:ne * nrb, 0:width].astype(I32)
        return carry

    lax.fori_loop(0, gb, batch_step, 0)


def _topk(afft, consts):
    b, ne, n = afft.shape
    cap = max(1, EC_FACTOR * n // N_EXPERTS)
    nrb = max(cap // RANK_BLOCK, 1)
    width = min(cap, RANK_BLOCK)
    cb = min(CUMSUM_BLOCK, n)
    gb = min(b, max(1, TOPK_ROWS // (ne * n)))
    assert b % gb == 0
    idx = pl.pallas_call(
        functools.partial(_topk_body, gb=gb, n=n, cap=cap),
        out_shape=jax.ShapeDtypeStruct((b, ne * nrb, width), I32),
        grid=(b // gb,),
        in_specs=[pl.BlockSpec((gb, ne, n), lambda bi: (bi, 0, 0)), _const_spec((cb, cb))],
        out_specs=pl.BlockSpec((gb, ne * nrb, width), lambda bi: (bi, 0, 0)),
        scratch_shapes=[pltpu.VMEM((gb * ne, n), F32), pltpu.VMEM((ne, SUBLANES, n), F32)],
        compiler_params=_cparams(("arbitrary",)),
        name="topk",
    )(afft, consts["ut"][:cb, :cb])
    return idx.reshape(b, ne, cap)


def _ffn_body(idx_ref, h2p_ref, affn_ref, wg_ref, wu_ref, wd_ref, y_ref, xs_ref, ws_ref, *, rows, e_axis):
    def gather(j, carry):
        for k in range(ROW_UNROLL):
            r = j * ROW_UNROLL + k
            t = idx_ref[0, 0, r]
            xs_ref[pl.ds(r, 1), :] = h2p_ref[0, pl.ds(t, 1), :]
            ws_ref[pl.ds(r, 1), :] = affn_ref[0, pl.ds(t, 1), :]
        return carry

    lax.fori_loop(0, rows // ROW_UNROLL, gather, 0)
    words = xs_ref[...]
    x_lo = lax.bitcast_convert_type(words << 16, F32).astype(BF16)
    x_hi = lax.bitcast_convert_type(words & jnp.uint32(0xFFFF0000), F32).astype(BF16)
    half = D_MODEL // 2

    def up_proj(w_ref):
        return (_mm(x_lo, w_ref[0, 0:half, :])
                + _mm(x_hi, w_ref[0, half:, :]))

    gate = up_proj(wg_ref)
    hidden = ((gate * jax.nn.sigmoid(gate)) * up_proj(wu_ref)).astype(BF16)
    y = _mm(hidden, wd_ref[0])
    lane = lax.broadcasted_iota(I32, (rows, LANES), 1)
    val = jnp.sum(jnp.where(lane == pl.program_id(e_axis), ws_ref[...], 0.0), axis=1, keepdims=True)
    y_ref[0, 0] = y * val


def _moe_ffn(idx_steps, h2p, affn, lw, group, expert_outer):
    ng, ne, rows = idx_steps.shape
    gtok = h2p.shape[1] * group
    h2p = h2p.reshape(ng, gtok, D_MODEL // 2)
    affn = affn.reshape(ng, gtok, LANES)
    if expert_outer:
        grid = (ne, ng)
        ge = lambda a, b_: (b_, a)
    else:
        grid = (ng, ne)
        ge = lambda a, b_: (a, b_)
    g_of = lambda a, b_: ge(a, b_)[0]
    e_of = lambda a, b_: ge(a, b_)[1]
    idx_flat = idx_steps.reshape(ng * ne, 1, rows)
    w_spec = lambda r, c: pl.BlockSpec((1, r, c), lambda a, b_: (e_of(a, b_), 0, 0))
    return pl.pallas_call(
        functools.partial(_ffn_body, rows=rows, e_axis=0 if expert_outer else 1),
        out_shape=jax.ShapeDtypeStruct((ng, ne, rows, D_MODEL), F32),
        grid=grid,
        in_specs=[pl.BlockSpec((1, 1, rows), lambda a, b_: (g_of(a, b_) * ne + e_of(a, b_), 0, 0),
                               memory_space=pltpu.SMEM),
                  pl.BlockSpec((1, gtok, D_MODEL // 2), lambda a, b_: (g_of(a, b_), 0, 0)),
                  pl.BlockSpec((1, gtok, LANES), lambda a, b_: (g_of(a, b_), 0, 0)),
                  w_spec(D_MODEL, D_EXPERT), w_spec(D_MODEL, D_EXPERT), w_spec(D_EXPERT, D_MODEL)],
        out_specs=pl.BlockSpec((1, 1, rows, D_MODEL), lambda a, b_: (g_of(a, b_), e_of(a, b_), 0, 0)),
        scratch_shapes=[pltpu.VMEM((rows, D_MODEL // 2), U32), pltpu.VMEM((rows, LANES), F32)],
        compiler_params=_cparams(("arbitrary", "arbitrary")),
        name="moe_ffn",
    )(idx_flat, h2p, affn, lw["w_gate"], lw["w_up"], lw["w_down"])


def _combine_body(idx_ref, y_ref, o_ref, *, rows):
    @pl.when(pl.program_id(1) == 0)
    def _zero():
        o_ref[...] = jnp.zeros(o_ref.shape, F32)

    def scatter(j, carry):
        toks = [idx_ref[0, 0, j * ROW_UNROLL + k] for k in range(ROW_UNROLL)]
        vals = [o_ref[0, pl.ds(toks[k], 1), :] + y_ref[0, 0, pl.ds(j * ROW_UNROLL + k, 1), :]
                for k in range(ROW_UNROLL)]
        for k in range(ROW_UNROLL):
            o_ref[0, pl.ds(toks[k], 1), :] = vals[k]
        return carry

    lax.fori_loop(0, rows // ROW_UNROLL, scatter, 0)


def _combine(idx_steps, ysel, gtok):
    ng, ne, rows = idx_steps.shape
    return pl.pallas_call(
        functools.partial(_combine_body, rows=rows),
        out_shape=jax.ShapeDtypeStruct((ng, gtok, D_MODEL), F32),
        grid=(ng, ne),
        in_specs=[pl.BlockSpec((1, 1, rows), lambda g, e: (g * ne + e, 0, 0), memory_space=pltpu.SMEM),
                  pl.BlockSpec((1, 1, rows, D_MODEL), lambda g, e: (g, e, 0, 0))],
        out_specs=pl.BlockSpec((1, gtok, D_MODEL), lambda g, e: (g, 0, 0)),
        compiler_params=_cparams(("arbitrary", "arbitrary")),
        name="moe_combine",
    )(idx_steps.reshape(ng * ne, 1, rows), ysel)


def _expert_choice(h2p, affn, afft, lw, consts, group, expert_outer):
    b, n, _ = h2p.shape
    idx = _topk(afft, consts)
    cap = idx.shape[-1]
    ng = b // group
    offs = (jnp.arange(b, dtype=I32) % group * n).reshape(ng, group, 1, 1)
    idx_steps = (idx.reshape(ng, group, N_EXPERTS, cap) + offs).transpose(0, 2, 1, 3).reshape(ng, N_EXPERTS, group * cap)
    ysel = _moe_ffn(idx_steps, h2p, affn, lw, group, expert_outer)
    moe = _combine(idx_steps, ysel, group * n)
    return moe.reshape(b, n, D_MODEL)


def _final_body(x1_ref, moe_ref, mod_ref, g_ref, o_ref):
    d = D_MODEL
    x = x1_ref[0] + mod_ref[0, :, 5 * d:6 * d] * moe_ref[0]
    o_ref[0] = _rms(x) * g_ref[...]


def _final_norm(x1, moe, mod_l, row_of_batch, g):
    b, n, d = x1.shape
    tm = TOKEN_TILE
    tok = pl.BlockSpec((1, tm, d), lambda bi, i: (bi, i, 0))
    return pl.pallas_call(
        _final_body,
        out_shape=jax.ShapeDtypeStruct((b, n, d), F32),
        grid=(b, n // tm),
        in_specs=[tok, tok, pl.BlockSpec((1, 1, 6 * d), lambda bi, i: (row_of_batch(bi), 0, 0)), _const_spec((1, d))],
        out_specs=tok,
        compiler_params=_cparams(("arbitrary", "arbitrary")),
        name="final_norm",
    )(x1, moe, mod_l, g)


def _rope_tables(n_lat):
    half = AXIS_DIM // 2
    inv_freq = ROPE_THETA ** (-jnp.arange(half, dtype=F32) / half)
    t = jnp.arange(n_lat, dtype=jnp.int32)
    row = (t // GRID_W).astype(F32)
    col = (t % GRID_W).astype(F32)
    lane = np.arange(LANES)
    dim = lane % HEAD_DIM
    use_col = jnp.asarray(dim >= AXIS_DIM)
    freq = inv_freq[jnp.asarray(dim % half)]
    pos = jnp.where(use_col[None, :], col[:, None], row[:, None])
    ang = pos * freq[None, :]
    sign = jnp.asarray(np.where(dim % AXIS_DIM < half, -1.0, 1.0), dtype=F32)
    return jnp.cos(ang), jnp.sin(ang) * sign[None, :]


def _block_diag_ones(width):
    seg = np.arange(width) // HEAD_DIM
    return jnp.asarray(seg[:, None] == seg[None, :], dtype=BF16)


def kernel(x_prompt, x_sample, cache_k, cache_v, c, c_ctx, w_mod, b_mod, norm1, norm2, w_in, q_norm, k_norm,
           conv_w, w_conv_out, w_attn_out, w_o, w_router, w_gate, w_up, w_down, final_norm):
    depth = w_mod.shape[0]
    bc, nc, d = x_prompt.shape
    bl, nl, _ = x_sample.shape
    p_len = cache_k.shape[2]
    assert d == D_MODEL and bl + 1 <= MOD_ROWS
    assert nc % TOKEN_TILE == 0 and nl % KEY_BLOCK == 0 and p_len % TOKEN_TILE == 0

    cmat = jnp.zeros((MOD_ROWS, d), F32).at[0].set(c_ctx).at[1:1 + bl].set(c)
    mods = _modulation(cmat, w_mod, b_mod).reshape(depth, MOD_ROWS, 1, 6 * d)

    cos_t, sin_t = _rope_tables(nl)
    ut = np.arange(CUMSUM_BLOCK)
    consts = {"bdq": _block_diag_ones(Q_W), "bdk": _block_diag_ones(KV_W), "cos": cos_t, "sin": sin_t,
              "ut": jnp.asarray(ut[:, None] <= ut[None, :], dtype=BF16)}

    kc_rep = jnp.tile(cache_k.transpose(0, 1, 3, 2, 4), (1, 1, 1, 1, GROUP)).astype(BF16)
    vc_t = cache_v.transpose(0, 1, 3, 4, 2).reshape(bl, depth, KV_W, p_len).astype(BF16)

    ctx_row = lambda bi: 0
    lat_row = lambda bi: bi + 1
    ctx_group = min(8, bc)

    state = {"ctx": (x_prompt, None), "lat": (x_sample, None)}
    new_k, new_v = [], []
    for l in range(depth):
        lw = {
            "norm1": norm1[l].reshape(1, d), "norm2": norm2[l].reshape(1, d),
            "w_in": w_in[l].astype(BF16),
            "gq": jnp.tile(q_norm[l], N_HEADS).reshape(1, Q_W), "gk": jnp.tile(k_norm[l], N_KV_HEADS).reshape(1, KV_W),
            "conv_w": jnp.zeros((SUBLANES, CONV_W), F32).at[0:3].set(conv_w[l]),
            "w_conv_out": w_conv_out[l].astype(BF16), "w_attn_out": w_attn_out[l].astype(BF16),
            "w_o": w_o[l].astype(BF16),
            "w_router": jnp.zeros((d, LANES), F32).at[:, 0:N_EXPERTS].set(w_router[l]),
            "w_gate": w_gate[l].astype(BF16), "w_up": w_up[l].astype(BF16), "w_down": w_down[l].astype(BF16),
        }
        mod_l = mods[l]
        modp = mods[l - 1] if l else None
        for name in ("ctx", "lat"):
            is_lat = name == "lat"
            row_fn = lat_row if is_lat else ctx_row
            xa, xb = state[name]
            res = _premix(xa, xb, modp if xb is not None else None, mod_l, row_fn, lw, consts, is_lat)
            q, krep, vt = res[0:3]
            pos = 3
            if not is_lat:
                new_k.append(res[3])
                new_v.append(res[4])
                pos = 5
            u, gb, sga, sgat = res[pos:pos + 4]
            x_cur = res[pos + 4] if xb is not None else xa
            if is_lat:
                attn = _attention(q, krep, vt, kc_rep[:, l], vc_t[:, l])
            else:
                attn = _attention(q, krep, vt)
            x1, h2p, affn, afft = _postmix(attn, u, gb, sga, sgat, x_cur, mod_l, row_fn, lw)
            moe = _expert_choice(h2p, affn, afft, lw, consts, group=1 if is_lat else ctx_group,
                                 expert_outer=not is_lat)
            state[name] = (x1, moe)

    g = final_norm.reshape(1, d)
    y_prompt = _final_norm(*state["ctx"], mods[depth - 1], ctx_row, g)
    y_sample = _final_norm(*state["lat"], mods[depth - 1], lat_row, g)
    shape_kv = (bc, depth, nc, N_KV_HEADS, HEAD_DIM)
    new_cache_k = jnp.stack(new_k, axis=1).reshape(shape_kv)
    new_cache_v = jnp.stack(new_v, axis=1).reshape(shape_kv)
    return (y_prompt, y_sample, new_cache_k, new_cache_v)
```

```python
import functools

import jax
import jax.numpy as jnp
import numpy as np
from jax import lax
from jax.experimental import pallas as pl
from jax.experimental.pallas import tpu as pltpu

F32 = jnp.float32
BF16 = jnp.bfloat16
I32 = jnp.int32
U32 = jnp.uint32

D_MODEL = 1024
N_HEADS = 16
N_KV_HEADS = 4
HEAD_DIM = 64
GROUP = N_HEADS // N_KV_HEADS
Q_W = N_HEADS * HEAD_DIM
KV_W = N_KV_HEADS * HEAD_DIM
CONV_W = D_MODEL // 2
IN_W = Q_W + 2 * KV_W + 3 * CONV_W + 2 * D_MODEL
N_EXPERTS = 16
EC_FACTOR = 2
D_EXPERT = 1024
GRID_W = 64
AXIS_DIM = HEAD_DIM // 2
ROPE_THETA = 10000.0
EPS = 1e-6
MOD_ROWS = 16

LANES = 128
SUBLANES = 8
VMEM_LIMIT_BYTES = 56 * 1024 * 1024

TOKEN_TILE = 256
KEY_BLOCK = 1024
QUERY_TILE = 512
CUMSUM_BLOCK = 256
RANK_BLOCK = 128
TOPK_ROWS = 512 * 1024
CTX_GROUP = 16
ONES_ROWS = 16
GATHER_W = D_MODEL // 2 + LANES
LOG2_E = 1.4426950408889634


def _cparams(sem):
    return pltpu.CompilerParams(dimension_semantics=sem, vmem_limit_bytes=VMEM_LIMIT_BYTES)


def _mm(a, b):
    return jnp.dot(a, b, preferred_element_type=F32)


def _const_spec(shape):
    nd = len(shape)
    return pl.BlockSpec(shape, lambda *_: (0,) * nd)


def _mod_body(c_ref, w_ref, b_ref, o_ref):
    c = c_ref[...]
    s = c * jax.nn.sigmoid(c)
    o_ref[0] = _mm(s.astype(BF16), w_ref[0].astype(BF16)) + b_ref[0]


def _modulation(cmat, w_mod, b_mod):
    depth = w_mod.shape[0]
    tn = 1536
    return pl.pallas_call(
        _mod_body,
        out_shape=jax.ShapeDtypeStruct((depth, MOD_ROWS, 6 * D_MODEL), F32),
        grid=(depth, 6 * D_MODEL // tn),
        in_specs=[
            pl.BlockSpec((MOD_ROWS, D_MODEL), lambda l, j: (0, 0)),
            pl.BlockSpec((1, D_MODEL, tn), lambda l, j: (l, 0, j)),
            pl.BlockSpec((1, 1, tn), lambda l, j: (l, 0, j)),
        ],
        out_specs=pl.BlockSpec((1, MOD_ROWS, tn), lambda l, j: (l, 0, j)),
        compiler_params=_cparams(("arbitrary", "arbitrary")),
        name="modulation",
    )(cmat, w_mod, b_mod.reshape(depth, 1, 6 * D_MODEL))


def _rms(x):
    return x * lax.rsqrt(jnp.mean(x * x, axis=-1, keepdims=True) + EPS)


def _premix_body(*refs, is_lat, has_prev, tm):
    it = iter(refs)
    xa_ref = next(it)
    if has_prev:
        xb_ref = next(it)
        modp_ref = next(it)
    mod_ref = next(it)
    n1_ref = next(it)
    win_ref = next(it)
    gq_ref = next(it)
    gk_ref = next(it)
    bdq_ref = next(it)
    bdk_ref = next(it)
    if is_lat:
        cos_ref = next(it)
        sin_ref = next(it)
    q_ref = next(it)
    krep_ref = next(it)
    vt_ref = next(it)
    if not is_lat:
        kc_ref = next(it)
        vc_ref = next(it)
    u_ref = next(it)
    gb_ref = next(it)
    sga_ref = next(it)
    sgat_ref = next(it)
    if has_prev:
        xn_ref = next(it)

    d = D_MODEL
    x = xa_ref[0]
    if has_prev:
        x = x + modp_ref[0, :, 5 * d:6 * d] * xb_ref[0]
        xn_ref[0] = x
    sh1 = mod_ref[0, :, 0:d]
    sc1 = mod_ref[0, :, d:2 * d]
    h = (_rms(x) * n1_ref[...]) * (1.0 + sc1) + sh1
    hb = h.astype(BF16)

    def proj(lo, hi):
        return _mm(hb, win_ref[:, lo:hi])

    lane = lax.broadcasted_iota(I32, (tm, LANES), 1)
    first_half = (lane % AXIS_DIM) < (AXIS_DIM // 2)
    low_head = lane < HEAD_DIM

    def rope(chunk, c):
        if not is_lat:
            return chunk
        del c
        partner = jnp.where(first_half, pltpu.roll(chunk, LANES - AXIS_DIM // 2, 1),
                            pltpu.roll(chunk, AXIS_DIM // 2, 1))
        return chunk * cos_ref[...] + partner * sin_ref[...]

    pq = proj(0, Q_W)
    ssq = _mm((pq * pq).astype(BF16), bdq_ref[...])
    qn = pq * lax.rsqrt(ssq * (1.0 / HEAD_DIM) + EPS) * gq_ref[...]
    for c in range(Q_W // LANES):
        qc = rope(qn[:, c * LANES:(c + 1) * LANES], c)
        q_ref[0, :, c * LANES:(c + 1) * LANES] = (qc * (HEAD_DIM ** -0.5 * LOG2_E)).astype(BF16)

    pk = proj(Q_W, Q_W + KV_W)
    ssk = _mm((pk * pk).astype(BF16), bdk_ref[...])
    kn = pk * lax.rsqrt(ssk * (1.0 / HEAD_DIM) + EPS) * gk_ref[...]
    if not is_lat:
        kc_ref[0] = kn
    for c in range(KV_W // LANES):
        kc = rope(kn[:, c * LANES:(c + 1) * LANES], c)
        rolled = pltpu.roll(kc, HEAD_DIM, 1)
        even = jnp.where(low_head, kc, rolled).astype(BF16)
        odd = jnp.where(low_head, rolled, kc).astype(BF16)
        for s in range(KV_W // LANES):
            krep_ref[0, 2 * c, :, s * LANES:(s + 1) * LANES] = even
            krep_ref[0, 2 * c + 1, :, s * LANES:(s + 1) * LANES] = odd

    pv = proj(Q_W + KV_W, Q_W + 2 * KV_W)
    if not is_lat:
        vc_ref[0] = pv
    vt_ref[0, 0] = pv.T.astype(BF16)

    o = Q_W + 2 * KV_W
    gb_ref[0] = proj(o, o + CONV_W).astype(BF16)
    u_ref[0] = proj(o + CONV_W, o + 2 * CONV_W) * proj(o + 2 * CONV_W, o + 3 * CONV_W)
    o = o + 3 * CONV_W
    sga_ref[0] = jax.nn.sigmoid(proj(o, o + d)).astype(BF16)
    sgat_ref[0] = jax.nn.sigmoid(proj(o + d, o + 2 * d)).astype(BF16)


def _premix(xa, xb, modp, mod_l, row_of_batch, lw, consts, is_lat):
    b, n, d = xa.shape
    tm = min(TOKEN_TILE, n)
    tk = min(KEY_BLOCK, n)
    has_prev = xb is not None
    sub = tk // tm
    tok_spec = lambda w: pl.BlockSpec((1, tm, w), lambda bi, i: (bi, i, 0))
    mod_spec = pl.BlockSpec((1, 1, 6 * d), lambda bi, i: (row_of_batch(bi), 0, 0))

    ins, specs = [xa], [tok_spec(d)]
    if has_prev:
        ins += [xb, modp]
        specs += [tok_spec(d), mod_spec]
    ins += [mod_l, lw["norm1"], lw["w_in"], lw["gq"], lw["gk"], consts["bdq"], consts["bdk"]]
    specs += [mod_spec, _const_spec((1, d)), _const_spec((d, IN_W)), _const_spec((1, Q_W)),
              _const_spec((1, KV_W)), _const_spec((Q_W, Q_W)), _const_spec((KV_W, KV_W))]
    if is_lat:
        ins += [consts["cos"], consts["sin"]]
        specs += [pl.BlockSpec((tm, LANES), lambda bi, i: (i, 0))] * 2

    outs = [jax.ShapeDtypeStruct((b, n, Q_W), BF16),
            jax.ShapeDtypeStruct((b, N_KV_HEADS, n, KV_W), BF16),
            jax.ShapeDtypeStruct((b, n // tk, KV_W, tk), BF16)]
    ospecs = [tok_spec(Q_W),
              pl.BlockSpec((1, N_KV_HEADS, tm, KV_W), lambda bi, i: (bi, 0, i, 0)),
              pl.BlockSpec((1, 1, KV_W, tm), lambda bi, i: (bi, i // sub, 0, i % sub))]
    if not is_lat:
        outs += [jax.ShapeDtypeStruct((b, n, KV_W), F32)] * 2
        ospecs += [tok_spec(KV_W)] * 2
    outs += [jax.ShapeDtypeStruct((b, n, CONV_W), F32), jax.ShapeDtypeStruct((b, n, CONV_W), BF16),
             jax.ShapeDtypeStruct((b, n, d), BF16), jax.ShapeDtypeStruct((b, n, d), BF16)]
    ospecs += [tok_spec(CONV_W), tok_spec(CONV_W), tok_spec(d), tok_spec(d)]
    if has_prev:
        outs.append(jax.ShapeDtypeStruct((b, n, d), F32))
        ospecs.append(tok_spec(d))

    res = pl.pallas_call(
        functools.partial(_premix_body, is_lat=is_lat, has_prev=has_prev, tm=tm),
        out_shape=outs, grid=(b, n // tm), in_specs=specs, out_specs=ospecs,
        compiler_params=_cparams(("arbitrary", "arbitrary")),
        name="premix_lat" if is_lat else "premix_ctx",
    )(*ins)
    return list(res)


def _attn_body(*refs, p_len, n, tq, tk):
    it = iter(refs)
    q_ref = next(it)
    if p_len:
        kc_ref = next(it)
        vc_ref = next(it)
    k_ref = next(it)
    v_ref = next(it)
    o_ref = next(it)
    qbd_ref = next(it)
    s_ref = next(it)
    m_ref = next(it)
    acc_ref = next(it)

    qt = q_ref[0].astype(F32).T
    head_of_row = lax.broadcasted_iota(I32, (KV_W, tq), 0) // HEAD_DIM
    for g in range(GROUP):
        qbd_ref[g] = jnp.where(head_of_row == g, qt, 0.0).astype(BF16)
    m_ref[...] = jnp.full(m_ref.shape, -jnp.inf, F32)
    acc_ref[...] = jnp.zeros(acc_ref.shape, F32)

    def score(parts, g):
        off, m_blk = 0, None
        for get_k, _, size in parts:
            s = _mm(get_k(), qbd_ref[g])
            s_ref[g, off:off + size, :] = s
            m_part = jnp.max(s, axis=0, keepdims=True)
            m_blk = m_part if m_blk is None else jnp.maximum(m_blk, m_part)
            off += size
        return m_blk

    def softmax_pv(parts, g, m_blk):
        size = sum(part[2] for part in parts)
        m_prev = m_ref[g]
        m_new = jnp.maximum(m_prev, m_blk)
        alpha = jnp.exp2(m_prev - m_new)
        p = jnp.exp2(s_ref[g, 0:size, :] - m_new).astype(BF16)
        v_ext = jnp.concatenate([jnp.concatenate([get_v() for _, get_v, _ in parts], axis=1),
                                 jnp.ones((ONES_ROWS, size), BF16)], axis=0)
        acc_ref[g] = alpha * acc_ref[g] + _mm(v_ext, p)
        m_ref[g] = m_new

    def key_block(parts, m_blk, next_parts):
        for g in range(GROUP):
            if g + 1 < GROUP:
                m_next = score(parts, g + 1)
            elif next_parts is not None:
                m_next = score(next_parts, 0)
            else:
                m_next = None
            softmax_pv(parts, g, m_blk)
            m_blk = m_next
        return m_blk

    def block_parts(i):
        if isinstance(i, int):
            parts = [(lambda: k_ref[0, 0, i * tk:(i + 1) * tk, :], lambda: v_ref[0, i], tk)]
            if p_len and i == 0:
                parts.append((lambda: kc_ref[0, 0], lambda: vc_ref[0], p_len))
            return parts
        return [(lambda: k_ref[0, 0, pl.ds(pl.multiple_of(i * tk, tk), tk), :], lambda: v_ref[0, i], tk)]

    nb = n // tk
    m_blk = score(block_parts(0), 0)
    if nb > 1:
        m_blk = key_block(block_parts(0), m_blk, block_parts(1))
        m_blk = lax.fori_loop(1, nb - 1, lambda i, m: key_block(block_parts(i), m, block_parts(i + 1)), m_blk)
    key_block(block_parts(nb - 1), m_blk, None)

    outs = []
    for g in range(GROUP):
        acc = acc_ref[g]
        outs.append(acc[0:HEAD_DIM, :] * (1.0 / acc[HEAD_DIM:HEAD_DIM + 1, :]))
    o_ref[0] = jnp.concatenate(outs, axis=0).T.astype(BF16)


def _attention(q, krep, vt, kc_rep=None, vc_t=None):
    b, n, _ = q.shape
    tk = vt.shape[-1]
    tq = min(QUERY_TILE, n)
    p_len = 0 if kc_rep is None else kc_rep.shape[2]
    ins, specs = [q], [pl.BlockSpec((1, tq, KV_W), lambda bi, h, i: (bi, i, h))]
    if p_len:
        ins += [kc_rep, vc_t]
        specs += [pl.BlockSpec((1, 1, p_len, KV_W), lambda bi, h, i: (bi, h, 0, 0)),
                  pl.BlockSpec((1, HEAD_DIM, p_len), lambda bi, h, i: (bi, h, 0))]
    ins += [krep, vt]
    specs += [pl.BlockSpec((1, 1, n, KV_W), lambda bi, h, i: (bi, h, 0, 0)),
              pl.BlockSpec((1, n // tk, HEAD_DIM, tk), lambda bi, h, i: (bi, 0, h, 0))]
    return pl.pallas_call(
        functools.partial(_attn_body, p_len=p_len, n=n, tq=tq, tk=tk),
        out_shape=jax.ShapeDtypeStruct((b, n, Q_W), BF16),
        grid=(b, N_KV_HEADS, n // tq),
        in_specs=specs,
        out_specs=pl.BlockSpec((1, tq, KV_W), lambda bi, h, i: (bi, i, h)),
        scratch_shapes=[pltpu.VMEM((GROUP, KV_W, tq), BF16),
                        pltpu.VMEM((GROUP, tk + p_len, tq), F32),
                        pltpu.VMEM((GROUP, 1, tq), F32),
                        pltpu.VMEM((GROUP, HEAD_DIM + ONES_ROWS, tq), F32)],
        compiler_params=_cparams(("arbitrary", "arbitrary", "arbitrary")),
        name="attention_lat" if p_len else "attention_ctx",
    )(*ins)


def _split_bf16(x):
    hi = x.astype(BF16)
    lo = (x - hi.astype(F32)).astype(BF16)
    return hi, lo


def _postmix_body(attn_ref, u_ref, up_ref, un_ref, gb_ref, sga_ref, sgat_ref, x_ref, mod_ref, n2_ref, cw_ref,
                  wa_ref, wc_ref, wo_ref, wr_ref, x1_ref, hx_ref, afft_ref, *, tm):
    d = D_MODEL
    i = pl.program_id(1)
    last = pl.num_programs(1) - 1
    u = u_ref[0]
    row = lax.broadcasted_iota(I32, (tm, CONV_W), 0)
    prev_row = jnp.where(i > 0, up_ref[0, SUBLANES - 1:SUBLANES, :], 0.0)
    next_row = jnp.where(i < last, un_ref[0, 0:1, :], 0.0)
    u_m1 = jnp.where(row == 0, prev_row, pltpu.roll(u, 1, 0))
    u_p1 = jnp.where(row == tm - 1, next_row, pltpu.roll(u, tm - 1, 0))
    conv = cw_ref[0:1, :] * u_m1 + cw_ref[1:2, :] * u + cw_ref[2:3, :] * u_p1
    cv = (gb_ref[0].astype(F32) * conv).astype(BF16)
    conv_o = _mm(cv, wc_ref[...])
    attn_o = _mm(attn_ref[0], wa_ref[...])
    merged = sga_ref[0].astype(F32) * conv_o + sgat_ref[0].astype(F32) * attn_o
    mix = _mm(merged.astype(BF16), wo_ref[...])
    x1 = x_ref[0] + mod_ref[0, :, 2 * d:3 * d] * mix
    x1_ref[0] = x1
    h2 = (_rms(x1) * n2_ref[...]) * (1.0 + mod_ref[0, :, 4 * d:5 * d]) + mod_ref[0, :, 3 * d:4 * d]

    h_hi, h_lo = _split_bf16(h2)
    half = d // 2
    w_lo = lax.bitcast_convert_type(h_hi[:, :half].astype(F32), U32) >> 16
    w_hi = lax.bitcast_convert_type(h_hi[:, half:].astype(F32), U32) & jnp.uint32(0xFFFF0000)
    hx_ref[0, :, 0:half] = w_lo | w_hi

    r_hi, r_lo = _split_bf16(wr_ref[...])
    logits = (_mm(h_hi, r_hi) + _mm(h_hi, r_lo)
              + _mm(h_lo, r_hi))
    lane = lax.broadcasted_iota(I32, (tm, LANES), 1)
    valid = lane < N_EXPERTS
    logits = jnp.where(valid, logits, -jnp.inf)
    e = jnp.exp(logits - jnp.max(logits, axis=-1, keepdims=True))
    e = jnp.where(valid, e, 0.0)
    aff = e / jnp.sum(e, axis=-1, keepdims=True)
    hx_ref[0, :, half:half + LANES] = lax.bitcast_convert_type(aff, U32)
    afft_ref[0] = aff.T[0:N_EXPERTS, :]


def _postmix(attn, u, gb, sga, sgat, x, mod_l, row_of_batch, lw):
    b, n, d = x.shape
    tm = min(TOKEN_TILE, n)
    nsub = n // SUBLANES
    per = tm // SUBLANES
    tok_spec = lambda w: pl.BlockSpec((1, tm, w), lambda bi, i: (bi, i, 0))
    specs = [tok_spec(Q_W), tok_spec(CONV_W),
             pl.BlockSpec((1, SUBLANES, CONV_W), lambda bi, i: (bi, jnp.maximum(i * per - 1, 0), 0)),
             pl.BlockSpec((1, SUBLANES, CONV_W), lambda bi, i: (bi, jnp.minimum((i + 1) * per, nsub - 1), 0)),
             tok_spec(CONV_W), tok_spec(d), tok_spec(d), tok_spec(d),
             pl.BlockSpec((1, 1, 6 * d), lambda bi, i: (row_of_batch(bi), 0, 0)),
             _const_spec((1, d)), _const_spec((SUBLANES, CONV_W)),
             _const_spec((Q_W, d)), _const_spec((CONV_W, d)), _const_spec((d, d)), _const_spec((d, LANES))]
    outs = [jax.ShapeDtypeStruct((b, n, d), F32), jax.ShapeDtypeStruct((b, n, GATHER_W), U32),
            jax.ShapeDtypeStruct((b, N_EXPERTS, n), F32)]
    ospecs = [tok_spec(d), tok_spec(GATHER_W),
              pl.BlockSpec((1, N_EXPERTS, tm), lambda bi, i: (bi, 0, i))]
    return pl.pallas_call(
        functools.partial(_postmix_body, tm=tm),
        out_shape=outs, grid=(b, n // tm), in_specs=specs, out_specs=ospecs,
        compiler_params=_cparams(("arbitrary", "arbitrary")),
        name="postmix",
    )(attn, u, u, u, gb, sga, sgat, x, mod_l, lw["norm2"], lw["conv_w"], lw["w_attn_out"], lw["w_conv_out"],
      lw["w_o"], lw["w_router"])


def _topk_body(aff_ref, ut_ref, idx_ref, sel_ref, pos_ref, *, gb, n, cap):
    ne = N_EXPERTS
    rows_all = gb * ne
    a = aff_ref[...].reshape(rows_all, n)
    bits = lax.bitcast_convert_type(a, I32)

    def count(mask):
        return jnp.sum(jnp.where(mask, 1.0, 0.0), axis=1, keepdims=True)

    def thr_step(t, thr):
        cand = thr | (jnp.int32(1) << (30 - t))
        return jnp.where(count(bits >= cand) >= cap, cand, thr)

    thr = lax.fori_loop(0, 31, thr_step, jnp.zeros((rows_all, 1), I32))
    gt = bits > thr
    eq = bits == thr
    need = cap - count(gt)
    tok = lax.broadcasted_iota(I32, (rows_all, n), 1)
    nbits = int(np.log2(n)) + 1

    def tie_step(t, bound):
        cand = bound + (jnp.int32(1) << (nbits - 1 - t))
        ok = (cand <= n) & (count(eq & (tok < cand)) <= need)
        return jnp.where(ok, cand, bound)

    bound = lax.fori_loop(0, nbits, tie_step, jnp.zeros((rows_all, 1), I32))
    sel_ref[...] = jnp.where(gt | (eq & (tok < bound)), 1.0, 0.0)

    cb = min(CUMSUM_BLOCK, n)
    nblk = n // cb
    nrb = max(cap // RANK_BLOCK, 1)
    width = min(cap, RANK_BLOCK)
    tiles = RANK_BLOCK // SUBLANES
    lane = lax.broadcasted_iota(I32, (RANK_BLOCK, LANES), 1)
    rank0 = (lax.broadcasted_iota(I32, (tiles, SUBLANES, LANES), 0) * SUBLANES
             + lax.broadcasted_iota(I32, (tiles, SUBLANES, LANES), 1)).astype(F32)

    def batch_step(bi, carry):
        sel = sel_ref[pl.ds(pl.multiple_of(bi * ne, ne), ne), :]
        off = jnp.zeros((ne, 1), F32)
        for k in range(nblk):
            blk = _mm(sel[:, k * cb:(k + 1) * cb].astype(BF16), ut_ref[...])
            pos = blk + off
            for ex in range(ne):
                pos_ref[ex, :, k * cb:(k + 1) * cb] = jnp.broadcast_to(pos[ex:ex + 1, :], (SUBLANES, cb))
            off = off + blk[:, cb - 1:cb]

        def expert_step(ei, cols):
            for rb in range(nrb):
                rank = rank0 + float(rb * RANK_BLOCK)
                cnt = jnp.zeros((tiles, SUBLANES, LANES), F32)
                for lc in range(n // LANES):
                    prow = pos_ref[ei, :, lc * LANES:(lc + 1) * LANES]
                    cnt = cnt + jnp.where(prow[None] <= rank, 1.0, 0.0)
                col = jnp.sum(cnt.reshape(RANK_BLOCK, LANES), axis=1, keepdims=True)
                cols = jnp.where(lane == ei * nrb + rb, col, cols)
            return cols

        cols = lax.fori_loop(0, ne, expert_step, jnp.zeros((RANK_BLOCK, LANES), F32))
        rows = cols.T
        idx_ref[bi] = rows[0:ne * nrb, 0:width].astype(I32)
        return carry

    lax.fori_loop(0, gb, batch_step, 0)


def _topk(afft, consts):
    b, ne, n = afft.shape
    cap = max(1, EC_FACTOR * n // N_EXPERTS)
    nrb = max(cap // RANK_BLOCK, 1)
    width = min(cap, RANK_BLOCK)
    cb = min(CUMSUM_BLOCK, n)
    gb = min(b, max(1, TOPK_ROWS // (ne * n)))
    assert b % gb == 0
    idx = pl.pallas_call(
        functools.partial(_topk_body, gb=gb, n=n, cap=cap),
        out_shape=jax.ShapeDtypeStruct((b, ne * nrb, width), I32),
        grid=(b // gb,),
        in_specs=[pl.BlockSpec((gb, ne, n), lambda bi: (bi, 0, 0)), _const_spec((cb, cb))],
        out_specs=pl.BlockSpec((gb, ne * nrb, width), lambda bi: (bi, 0, 0)),
        scratch_shapes=[pltpu.VMEM((gb * ne, n), F32), pltpu.VMEM((ne, SUBLANES, n), F32)],
        compiler_params=_cparams(("arbitrary",)),
        name="topk",
    )(afft, consts["ut"][:cb, :cb])
    return idx.reshape(b, ne, cap)


def _ffn_body(idx_ref, idx_next_ref, hx_ref, wg_ref, wu_ref, wd_ref, y_ref, xs_ref, *, rows):
    e = pl.program_id(1)

    def gather_row(src_ref, slot, j, k):
        t = src_ref[0, 0, j * SUBLANES + k]
        xs_ref[slot, j, pl.ds(k, 1), :] = hx_ref[0, pl.ds(t, 1), :]

    @pl.when(e == 0)
    def _gather_first_expert():
        def body(j, carry):
            for k in range(SUBLANES):
                gather_row(idx_ref, 0, j, k)
            return carry
        lax.fori_loop(0, rows // SUBLANES, body, 0)

    slot = lax.rem(e, 2)
    half = D_MODEL // 2
    gathered = xs_ref[slot].reshape(rows, GATHER_W)
    for j in range(rows // SUBLANES):
        for k in range(SUBLANES):
            gather_row(idx_next_ref, 1 - slot, j, k)
    words = gathered[:, 0:half]
    aff = lax.bitcast_convert_type(gathered[:, half:half + LANES], F32)
    x_lo = lax.bitcast_convert_type(words << 16, F32).astype(BF16)
    x_hi = lax.bitcast_convert_type(words & jnp.uint32(0xFFFF0000), F32).astype(BF16)

    def up_proj(w_ref):
        return (_mm(x_lo, w_ref[0, 0, 0:half, :])
                + _mm(x_hi, w_ref[0, 0, half:, :]))

    gate = up_proj(wg_ref)
    hidden = ((gate * jax.nn.sigmoid(gate)) * up_proj(wu_ref)).astype(BF16)
    y = _mm(hidden, wd_ref[0, 0])
    lane = lax.broadcasted_iota(I32, (rows, LANES), 1)
    val = jnp.sum(jnp.where(lane == e, aff, 0.0), axis=1, keepdims=True)
    y_ref[0, 0] = (y * val).reshape(rows // SUBLANES, SUBLANES, D_MODEL)


def _moe_ffn(idx_steps, hx, lw, group):
    ng, ne, rows = idx_steps.shape
    gtok = hx.shape[1] * group
    hx = hx.reshape(ng, gtok, GATHER_W)
    idx_flat = idx_steps.reshape(ng * ne, 1, rows)
    layer = lw["layer"]
    w_spec = lambda r, c: pl.BlockSpec((1, 1, r, c), lambda g, e: (layer, e, 0, 0))
    idx_spec = lambda step: pl.BlockSpec((1, 1, rows), lambda g, e: (g * ne + jnp.minimum(e + step, ne - 1), 0, 0),
                                         memory_space=pltpu.SMEM)
    return pl.pallas_call(
        functools.partial(_ffn_body, rows=rows),
        out_shape=jax.ShapeDtypeStruct((ng, ne, rows // SUBLANES, SUBLANES, D_MODEL), F32),
        grid=(ng, ne),
        in_specs=[idx_spec(0), idx_spec(1), pl.BlockSpec((1, gtok, GATHER_W), lambda g, e: (g, 0, 0)),
                  w_spec(D_MODEL, D_EXPERT), w_spec(D_MODEL, D_EXPERT), w_spec(D_EXPERT, D_MODEL)],
        out_specs=pl.BlockSpec((1, 1, rows // SUBLANES, SUBLANES, D_MODEL), lambda g, e: (g, e, 0, 0, 0)),
        scratch_shapes=[pltpu.VMEM((2, rows // SUBLANES, SUBLANES, GATHER_W), U32)],
        compiler_params=_cparams(("arbitrary", "arbitrary")),
        name="moe_ffn",
    )(idx_flat, idx_flat, hx, lw["w_gate"], lw["w_up"], lw["w_down"])


def _combine_body(idx_ref, y_ref, o_ref, *, rows):
    @pl.when(pl.program_id(1) == 0)
    def _zero():
        o_ref[...] = jnp.zeros(o_ref.shape, F32)

    def scatter(j, carry):
        toks = [idx_ref[0, 0, j * SUBLANES + k] for k in range(SUBLANES)]
        vals = [o_ref[0, pl.ds(toks[k], 1), :] + y_ref[0, 0, j, pl.ds(k, 1), :] for k in range(SUBLANES)]
        for k in range(SUBLANES):
            o_ref[0, pl.ds(toks[k], 1), :] = vals[k]
        return carry

    lax.fori_loop(0, rows // SUBLANES, scatter, 0)


def _combine(idx_steps, ysel, gtok):
    ng, ne, rows = idx_steps.shape
    return pl.pallas_call(
        functools.partial(_combine_body, rows=rows),
        out_shape=jax.ShapeDtypeStruct((ng, gtok, D_MODEL), F32),
        grid=(ng, ne),
        in_specs=[pl.BlockSpec((1, 1, rows), lambda g, e: (g * ne + e, 0, 0), memory_space=pltpu.SMEM),
                  pl.BlockSpec((1, 1, rows // SUBLANES, SUBLANES, D_MODEL), lambda g, e: (g, e, 0, 0, 0))],
        out_specs=pl.BlockSpec((1, gtok, D_MODEL), lambda g, e: (g, 0, 0)),
        compiler_params=_cparams(("arbitrary", "arbitrary")),
        name="moe_combine",
    )(idx_steps.reshape(ng * ne, 1, rows), ysel)


def _expert_choice(hx, afft, lw, consts, group):
    b, n, _ = hx.shape
    idx = _topk(afft, consts)
    cap = idx.shape[-1]
    ng = b // group
    offs = (jnp.arange(b, dtype=I32) % group * n).reshape(ng, group, 1, 1)
    idx_steps = (idx.reshape(ng, group, N_EXPERTS, cap) + offs).transpose(0, 2, 1, 3).reshape(ng, N_EXPERTS, group * cap)
    ysel = _moe_ffn(idx_steps, hx, lw, group)
    moe = _combine(idx_steps, ysel, group * n)
    return moe.reshape(b, n, D_MODEL)


def _final_body(x1_ref, moe_ref, mod_ref, g_ref, o_ref):
    d = D_MODEL
    x = x1_ref[0] + mod_ref[0, :, 5 * d:6 * d] * moe_ref[0]
    o_ref[0] = _rms(x) * g_ref[...]


def _final_norm(x1, moe, mod_l, row_of_batch, g):
    b, n, d = x1.shape
    tm = min(TOKEN_TILE, n)
    tok = pl.BlockSpec((1, tm, d), lambda bi, i: (bi, i, 0))
    return pl.pallas_call(
        _final_body,
        out_shape=jax.ShapeDtypeStruct((b, n, d), F32),
        grid=(b, n // tm),
        in_specs=[tok, tok, pl.BlockSpec((1, 1, 6 * d), lambda bi, i: (row_of_batch(bi), 0, 0)), _const_spec((1, d))],
        out_specs=tok,
        compiler_params=_cparams(("arbitrary", "arbitrary")),
        name="final_norm",
    )(x1, moe, mod_l, g)


def _rope_tables(n_lat):
    half = AXIS_DIM // 2
    inv_freq = ROPE_THETA ** (-jnp.arange(half, dtype=F32) / half)
    t = jnp.arange(n_lat, dtype=jnp.int32)
    row = (t // GRID_W).astype(F32)
    col = (t % GRID_W).astype(F32)
    lane = np.arange(LANES)
    dim = lane % HEAD_DIM
    use_col = jnp.asarray(dim >= AXIS_DIM)
    freq = inv_freq[jnp.asarray(dim % half)]
    pos = jnp.where(use_col[None, :], col[:, None], row[:, None])
    ang = pos * freq[None, :]
    sign = jnp.asarray(np.where(dim % AXIS_DIM < half, -1.0, 1.0), dtype=F32)
    return jnp.cos(ang), jnp.sin(ang) * sign[None, :]


def _block_diag_ones(width):
    seg = np.arange(width) // HEAD_DIM
    return jnp.asarray(seg[:, None] == seg[None, :], dtype=BF16)


def kernel(x_prompt, x_sample, cache_k, cache_v, c, c_ctx, w_mod, b_mod, norm1, norm2, w_in, q_norm, k_norm,
           conv_w, w_conv_out, w_attn_out, w_o, w_router, w_gate, w_up, w_down, final_norm):
    depth = w_mod.shape[0]
    bc, nc, d = x_prompt.shape
    bl, nl, _ = x_sample.shape
    p_len = cache_k.shape[2]
    assert d == D_MODEL and bl + 1 <= MOD_ROWS
    assert nc % min(TOKEN_TILE, nc) == 0 and nl % KEY_BLOCK == 0 and p_len % LANES == 0

    cmat = jnp.zeros((MOD_ROWS, d), F32).at[0].set(c_ctx).at[1:1 + bl].set(c)
    mods = _modulation(cmat, w_mod, b_mod).reshape(depth, MOD_ROWS, 1, 6 * d)

    cos_t, sin_t = _rope_tables(nl)
    ut = np.arange(CUMSUM_BLOCK)
    consts = {"bdq": _block_diag_ones(Q_W), "bdk": _block_diag_ones(KV_W), "cos": cos_t, "sin": sin_t,
              "ut": jnp.asarray(ut[:, None] <= ut[None, :], dtype=BF16)}

    kc_rep = jnp.tile(cache_k.transpose(0, 1, 3, 2, 4), (1, 1, 1, 1, GROUP)).astype(BF16)
    vc_t = cache_v.transpose(0, 1, 3, 4, 2).reshape(bl, depth, KV_W, p_len).astype(BF16)

    ctx_row = lambda bi: 0
    lat_row = lambda bi: bi + 1
    ctx_group = min(CTX_GROUP, bc)

    w_gate_b, w_up_b, w_down_b = w_gate.astype(BF16), w_up.astype(BF16), w_down.astype(BF16)
    state = {"ctx": (x_prompt, None), "lat": (x_sample, None)}
    new_k, new_v = [], []
    for l in range(depth):
        lw = {
            "norm1": norm1[l].reshape(1, d), "norm2": norm2[l].reshape(1, d),
            "w_in": w_in[l].astype(BF16),
            "gq": jnp.tile(q_norm[l], N_HEADS).reshape(1, Q_W), "gk": jnp.tile(k_norm[l], N_KV_HEADS).reshape(1, KV_W),
            "conv_w": jnp.zeros((SUBLANES, CONV_W), F32).at[0:3].set(conv_w[l]),
            "w_conv_out": w_conv_out[l].astype(BF16), "w_attn_out": w_attn_out[l].astype(BF16),
            "w_o": w_o[l].astype(BF16),
            "w_router": jnp.zeros((d, LANES), F32).at[:, 0:N_EXPERTS].set(w_router[l]),
            "w_gate": w_gate_b, "w_up": w_up_b, "w_down": w_down_b, "layer": l,
        }
        mod_l = mods[l]
        modp = mods[l - 1] if l else None
        for name in ("ctx", "lat"):
            is_lat = name == "lat"
            row_fn = lat_row if is_lat else ctx_row
            xa, xb = state[name]
            res = _premix(xa, xb, modp if xb is not None else None, mod_l, row_fn, lw, consts, is_lat)
            q, krep, vt = res[0:3]
            pos = 3
            if not is_lat:
                new_k.append(res[3])
                new_v.append(res[4])
                pos = 5
            u, gb, sga, sgat = res[pos:pos + 4]
            x_cur = res[pos + 4] if xb is not None else xa
            if is_lat:
                attn = _attention(q, krep, vt, kc_rep[:, l], vc_t[:, l])
            else:
                attn = _attention(q, krep, vt)
            x1, hx, afft = _postmix(attn, u, gb, sga, sgat, x_cur, mod_l, row_fn, lw)
            moe = _expert_choice(hx, afft, lw, consts, group=1 if is_lat else ctx_group)
            state[name] = (x1, moe)

    g = final_norm.reshape(1, d)
    y_prompt = _final_norm(*state["ctx"], mods[depth - 1], ctx_row, g)
    y_sample = _final_norm(*state["lat"], mods[depth - 1], lat_row, g)
    shape_kv = (bc, depth, nc, N_KV_HEADS, HEAD_DIM)
    new_cache_k = jnp.stack(new_k, axis=1).reshape(shape_kv)
    new_cache_v = jnp.stack(new_v, axis=1).reshape(shape_kv)
    return (y_prompt, y_sample, new_cache_k, new_cache_v)
```

```python
import functools

import jax
import jax.numpy as jnp
import numpy as np
from jax import lax
from jax.experimental import pallas as pl
from jax.experimental.pallas import tpu as pltpu

F32 = jnp.float32
BF16 = jnp.bfloat16
I32 = jnp.int32
U32 = jnp.uint32

D_MODEL = 1024
N_HEADS = 16
N_KV_HEADS = 4
HEAD_DIM = 64
GROUP = N_HEADS // N_KV_HEADS
Q_W = N_HEADS * HEAD_DIM
KV_W = N_KV_HEADS * HEAD_DIM
CONV_W = D_MODEL // 2
IN_W = Q_W + 2 * KV_W + 3 * CONV_W + 2 * D_MODEL
N_EXPERTS = 16
EC_FACTOR = 2
D_EXPERT = 1024
GRID_W = 64
AXIS_DIM = HEAD_DIM // 2
ROPE_THETA = 10000.0
EPS = 1e-6
MOD_ROWS = 16

LANES = 128
SUBLANES = 8
VMEM_LIMIT_BYTES = 56 * 1024 * 1024

MIX_TILE = 512
ROW_CHAIN = 256
FINAL_TILE = 1024
KEY_BLOCK = 1024
QUERY_TILE = 512
CUMSUM_BLOCK = 256
RANK_BLOCK = 128
TOPK_ROWS = 512 * 1024
CTX_GROUP = 16
ONES_ROWS = 16
GATHER_W = D_MODEL // 2 + LANES
LOG2_E = 1.4426950408889634


def _cparams(sem):
    return pltpu.CompilerParams(dimension_semantics=sem, vmem_limit_bytes=VMEM_LIMIT_BYTES)


def _mm(a, b):
    return jnp.dot(a, b, preferred_element_type=F32)


def _const_spec(shape, single=False):
    nd = len(shape)
    return pl.BlockSpec(shape, lambda *_: (0,) * nd, pipeline_mode=pl.Buffered(1) if single else None)


def _mod_body(c_ref, w_ref, b_ref, o_ref):
    c = c_ref[...]
    s = c * jax.nn.sigmoid(c)
    o_ref[0] = _mm(s.astype(BF16), w_ref[0].astype(BF16)) + b_ref[0]


def _modulation(cmat, w_mod, b_mod):
    depth = w_mod.shape[0]
    tn = 1536
    return pl.pallas_call(
        _mod_body,
        out_shape=jax.ShapeDtypeStruct((depth, MOD_ROWS, 6 * D_MODEL), F32),
        grid=(depth, 6 * D_MODEL // tn),
        in_specs=[
            pl.BlockSpec((MOD_ROWS, D_MODEL), lambda l, j: (0, 0)),
            pl.BlockSpec((1, D_MODEL, tn), lambda l, j: (l, 0, j)),
            pl.BlockSpec((1, 1, tn), lambda l, j: (l, 0, j)),
        ],
        out_specs=pl.BlockSpec((1, MOD_ROWS, tn), lambda l, j: (l, 0, j)),
        compiler_params=_cparams(("arbitrary", "arbitrary")),
        name="modulation",
    )(cmat, w_mod, b_mod.reshape(depth, 1, 6 * D_MODEL))


def _rms(x):
    return x * lax.rsqrt(jnp.mean(x * x, axis=-1, keepdims=True) + EPS)


def _premix_body(*refs, is_lat, has_prev, tm):
    it = iter(refs)
    xa_ref = next(it)
    if has_prev:
        xb_ref = next(it)
        modp_ref = next(it)
    mod_ref = next(it)
    n1_ref = next(it)
    win_ref = next(it)
    gq_ref = next(it)
    gk_ref = next(it)
    bdq_ref = next(it)
    bdk_ref = next(it)
    if is_lat:
        cos_ref = next(it)
        sin_ref = next(it)
    q_ref = next(it)
    krep_ref = next(it)
    vt_ref = next(it)
    if not is_lat:
        kc_ref = next(it)
        vc_ref = next(it)
    u_ref = next(it)
    gb_ref = next(it)
    sga_ref = next(it)
    sgat_ref = next(it)
    if has_prev:
        xn_ref = next(it)

    d = D_MODEL
    sub = min(ROW_CHAIN, tm)
    lane = lax.broadcasted_iota(I32, (sub, LANES), 1)
    first_half = (lane % AXIS_DIM) < (AXIS_DIM // 2)
    low_head = lane < HEAD_DIM
    sh1 = mod_ref[0, :, 0:d]
    sc1 = mod_ref[0, :, d:2 * d]

    def row_chain(rows):
        x = xa_ref[0, rows, :]
        if has_prev:
            x = x + modp_ref[0, :, 5 * d:6 * d] * xb_ref[0, rows, :]
            xn_ref[0, rows, :] = x
        h = (_rms(x) * n1_ref[...]) * (1.0 + sc1) + sh1
        hb = h.astype(BF16)

        def proj(lo, hi):
            return _mm(hb, win_ref[:, lo:hi])

        def rope(chunk):
            if not is_lat:
                return chunk
            partner = jnp.where(first_half, pltpu.roll(chunk, LANES - AXIS_DIM // 2, 1),
                                pltpu.roll(chunk, AXIS_DIM // 2, 1))
            return chunk * cos_ref[rows, :] + partner * sin_ref[rows, :]

        pq = proj(0, Q_W)
        ssq = _mm((pq * pq).astype(BF16), bdq_ref[...])
        qn = pq * lax.rsqrt(ssq * (1.0 / HEAD_DIM) + EPS) * gq_ref[...]
        for c in range(Q_W // LANES):
            qc = rope(qn[:, c * LANES:(c + 1) * LANES])
            q_ref[0, rows, c * LANES:(c + 1) * LANES] = (qc * (HEAD_DIM ** -0.5 * LOG2_E)).astype(BF16)

        pk = proj(Q_W, Q_W + KV_W)
        ssk = _mm((pk * pk).astype(BF16), bdk_ref[...])
        kn = pk * lax.rsqrt(ssk * (1.0 / HEAD_DIM) + EPS) * gk_ref[...]
        if not is_lat:
            kc_ref[0, rows, :] = kn
        for c in range(KV_W // LANES):
            kc = rope(kn[:, c * LANES:(c + 1) * LANES])
            rolled = pltpu.roll(kc, HEAD_DIM, 1)
            even = jnp.where(low_head, kc, rolled).astype(BF16)
            odd = jnp.where(low_head, rolled, kc).astype(BF16)
            for s in range(KV_W // LANES):
                krep_ref[0, 2 * c, rows, s * LANES:(s + 1) * LANES] = even
                krep_ref[0, 2 * c + 1, rows, s * LANES:(s + 1) * LANES] = odd

        pv = proj(Q_W + KV_W, Q_W + 2 * KV_W)
        if not is_lat:
            vc_ref[0, rows, :] = pv
        vt_ref[0, 0, :, rows] = pv.T.astype(BF16)

        o = Q_W + 2 * KV_W
        gb_ref[0, rows, :] = proj(o, o + CONV_W).astype(BF16)
        u_ref[0, rows, :] = proj(o + CONV_W, o + 2 * CONV_W) * proj(o + 2 * CONV_W, o + 3 * CONV_W)
        o = o + 3 * CONV_W
        sga_ref[0, rows, :] = jax.nn.sigmoid(proj(o, o + d)).astype(BF16)
        sgat_ref[0, rows, :] = jax.nn.sigmoid(proj(o + d, o + 2 * d)).astype(BF16)

    for c in range(tm // sub):
        row_chain(slice(c * sub, (c + 1) * sub))


def _premix(xa, xb, modp, mod_l, row_of_batch, lw, consts, is_lat):
    b, n, d = xa.shape
    tm = min(MIX_TILE, n)
    tk = min(KEY_BLOCK, n)
    has_prev = xb is not None
    sub = tk // tm
    tok_spec = lambda w: pl.BlockSpec((1, tm, w), lambda bi, i: (bi, i, 0))
    mod_spec = pl.BlockSpec((1, 1, 6 * d), lambda bi, i: (row_of_batch(bi), 0, 0))

    ins, specs = [xa], [tok_spec(d)]
    if has_prev:
        ins += [xb, modp]
        specs += [tok_spec(d), mod_spec]
    ins += [mod_l, lw["norm1"], lw["w_in"], lw["gq"], lw["gk"], consts["bdq"], consts["bdk"]]
    specs += [mod_spec, _const_spec((1, d)), _const_spec((d, IN_W), single=True), _const_spec((1, Q_W)),
              _const_spec((1, KV_W)), _const_spec((Q_W, Q_W), single=True), _const_spec((KV_W, KV_W))]
    if is_lat:
        ins += [consts["cos"], consts["sin"]]
        specs += [pl.BlockSpec((tm, LANES), lambda bi, i: (i, 0))] * 2

    outs = [jax.ShapeDtypeStruct((b, n, Q_W), BF16),
            jax.ShapeDtypeStruct((b, N_KV_HEADS, n, KV_W), BF16),
            jax.ShapeDtypeStruct((b, n // tk, KV_W, tk), BF16)]
    ospecs = [tok_spec(Q_W),
              pl.BlockSpec((1, N_KV_HEADS, tm, KV_W), lambda bi, i: (bi, 0, i, 0)),
              pl.BlockSpec((1, 1, KV_W, tm), lambda bi, i: (bi, i // sub, 0, i % sub))]
    if not is_lat:
        outs += [jax.ShapeDtypeStruct((b, n, KV_W), F32)] * 2
        ospecs += [tok_spec(KV_W)] * 2
    outs += [jax.ShapeDtypeStruct((b, n, CONV_W), F32), jax.ShapeDtypeStruct((b, n, CONV_W), BF16),
             jax.ShapeDtypeStruct((b, n, d), BF16), jax.ShapeDtypeStruct((b, n, d), BF16)]
    ospecs += [tok_spec(CONV_W), tok_spec(CONV_W), tok_spec(d), tok_spec(d)]
    if has_prev:
        outs.append(jax.ShapeDtypeStruct((b, n, d), F32))
        ospecs.append(tok_spec(d))

    res = pl.pallas_call(
        functools.partial(_premix_body, is_lat=is_lat, has_prev=has_prev, tm=tm),
        out_shape=outs, grid=(b, n // tm), in_specs=specs, out_specs=ospecs,
        compiler_params=_cparams(("arbitrary", "arbitrary")),
        name="premix_lat" if is_lat else "premix_ctx",
    )(*ins)
    return list(res)


def _attn_body(*refs, p_len, n, tq, tk, hps):
    it = iter(refs)
    q_ref = next(it)
    if p_len:
        kc_ref = next(it)
        vc_ref = next(it)
    k_ref = next(it)
    v_ref = next(it)
    o_ref = next(it)
    qbd_ref = next(it)
    s_ref = next(it)
    m_ref = next(it)
    acc_ref = next(it)

    m_ref[...] = jnp.full(m_ref.shape, -jnp.inf, F32)
    acc_ref[...] = jnp.zeros(acc_ref.shape, F32)
    nb = n // tk

    def one_kv_head(hh):
        base = hh * GROUP
        vrows = slice(hh * HEAD_DIM, (hh + 1) * HEAD_DIM)
        lanes = slice(hh * KV_W, (hh + 1) * KV_W)

        qt = q_ref[0, :, lanes].astype(F32).T
        head_of_row = lax.broadcasted_iota(I32, (KV_W, tq), 0) // HEAD_DIM
        for g in range(GROUP):
            qbd_ref[base + g] = jnp.where(head_of_row == g, qt, 0.0).astype(BF16)

        def score(parts, g):
            off, m_blk = 0, None
            for get_k, _, size in parts:
                s = _mm(get_k(), qbd_ref[base + g])
                s_ref[base + g, off:off + size, :] = s
                m_part = jnp.max(s, axis=0, keepdims=True)
                m_blk = m_part if m_blk is None else jnp.maximum(m_blk, m_part)
                off += size
            return m_blk

        def softmax_pv(parts, g, m_blk):
            size = sum(part[2] for part in parts)
            m_prev = m_ref[base + g]
            m_new = jnp.maximum(m_prev, m_blk)
            alpha = jnp.exp2(m_prev - m_new)
            p = jnp.exp2(s_ref[base + g, 0:size, :] - m_new).astype(BF16)
            v_ext = jnp.concatenate([jnp.concatenate([get_v() for _, get_v, _ in parts], axis=1),
                                     jnp.ones((ONES_ROWS, size), BF16)], axis=0)
            acc_ref[base + g] = alpha * acc_ref[base + g] + _mm(v_ext, p)
            m_ref[base + g] = m_new

        def key_block(parts, m_blk, next_parts):
            for g in range(GROUP):
                if g + 1 < GROUP:
                    m_next = score(parts, g + 1)
                elif next_parts is not None:
                    m_next = score(next_parts, 0)
                else:
                    m_next = None
                softmax_pv(parts, g, m_blk)
                m_blk = m_next
            return m_blk

        def block_parts(i):
            if isinstance(i, int):
                parts = [(lambda: k_ref[0, hh, i * tk:(i + 1) * tk, :], lambda: v_ref[0, i, vrows, :], tk)]
                if p_len and i == 0:
                    parts.append((lambda: kc_ref[0, hh], lambda: vc_ref[0, vrows, :], p_len))
                return parts
            return [(lambda: k_ref[0, hh, pl.ds(pl.multiple_of(i * tk, tk), tk), :], lambda: v_ref[0, i, vrows, :], tk)]

        m_blk = score(block_parts(0), 0)
        if nb > 1:
            m_blk = key_block(block_parts(0), m_blk, block_parts(1))
            m_blk = lax.fori_loop(1, nb - 1, lambda i, m: key_block(block_parts(i), m, block_parts(i + 1)), m_blk)
        key_block(block_parts(nb - 1), m_blk, None)

        outs = []
        for g in range(GROUP):
            acc = acc_ref[base + g]
            outs.append(acc[0:HEAD_DIM, :] * (1.0 / acc[HEAD_DIM:HEAD_DIM + 1, :]))
        o_ref[0, :, lanes] = jnp.concatenate(outs, axis=0).T.astype(BF16)

    for hh in range(hps):
        one_kv_head(hh)


def _attention(q, krep, vt, kc_rep=None, vc_t=None):
    b, n, _ = q.shape
    tk = vt.shape[-1]
    tq = min(QUERY_TILE, n)
    p_len = 0 if kc_rep is None else kc_rep.shape[2]
    hps = N_KV_HEADS if n == tk else 1
    ins, specs = [q], [pl.BlockSpec((1, tq, hps * KV_W), lambda bi, h, i: (bi, i, h))]
    if p_len:
        ins += [kc_rep, vc_t]
        specs += [pl.BlockSpec((1, hps, p_len, KV_W), lambda bi, h, i: (bi, h, 0, 0)),
                  pl.BlockSpec((1, hps * HEAD_DIM, p_len), lambda bi, h, i: (bi, h, 0))]
    ins += [krep, vt]
    specs += [pl.BlockSpec((1, hps, n, KV_W), lambda bi, h, i: (bi, h, 0, 0)),
              pl.BlockSpec((1, n // tk, hps * HEAD_DIM, tk), lambda bi, h, i: (bi, 0, h, 0))]
    return pl.pallas_call(
        functools.partial(_attn_body, p_len=p_len, n=n, tq=tq, tk=tk, hps=hps),
        out_shape=jax.ShapeDtypeStruct((b, n, Q_W), BF16),
        grid=(b, N_KV_HEADS // hps, n // tq),
        in_specs=specs,
        out_specs=pl.BlockSpec((1, tq, hps * KV_W), lambda bi, h, i: (bi, i, h)),
        scratch_shapes=[pltpu.VMEM((hps * GROUP, KV_W, tq), BF16),
                        pltpu.VMEM((hps * GROUP, tk + p_len, tq), F32),
                        pltpu.VMEM((hps * GROUP, 1, tq), F32),
                        pltpu.VMEM((hps * GROUP, HEAD_DIM + ONES_ROWS, tq), F32)],
        compiler_params=_cparams(("arbitrary", "arbitrary", "arbitrary")),
        name="attention_lat" if p_len else "attention_ctx",
    )(*ins)


def _split_bf16(x):
    hi = x.astype(BF16)
    lo = (x - hi.astype(F32)).astype(BF16)
    return hi, lo


def _postmix_body(attn_ref, u_ref, up_ref, un_ref, gb_ref, sga_ref, sgat_ref, x_ref, mod_ref, n2_ref, cw_ref,
                  wa_ref, wc_ref, wo_ref, wr_ref, x1_ref, hx_ref, afft_ref, *, tm):
    d = D_MODEL
    i = pl.program_id(1)
    last = pl.num_programs(1) - 1
    u = u_ref[0]
    row = lax.broadcasted_iota(I32, (tm, CONV_W), 0)
    prev_row = jnp.where(i > 0, up_ref[0, SUBLANES - 1:SUBLANES, :], 0.0)
    next_row = jnp.where(i < last, un_ref[0, 0:1, :], 0.0)
    u_m1 = jnp.where(row == 0, prev_row, pltpu.roll(u, 1, 0))
    u_p1 = jnp.where(row == tm - 1, next_row, pltpu.roll(u, tm - 1, 0))
    conv = cw_ref[0:1, :] * u_m1 + cw_ref[1:2, :] * u + cw_ref[2:3, :] * u_p1
    cv = (gb_ref[0].astype(F32) * conv).astype(BF16)
    r_hi, r_lo = _split_bf16(wr_ref[...])
    half = d // 2
    sub = min(ROW_CHAIN, tm)
    lane = lax.broadcasted_iota(I32, (sub, LANES), 1)
    valid = lane < N_EXPERTS

    for c in range(tm // sub):
        rows = slice(c * sub, (c + 1) * sub)
        conv_o = _mm(cv[rows], wc_ref[...])
        attn_o = _mm(attn_ref[0, rows, :], wa_ref[...])
        merged = sga_ref[0, rows, :].astype(F32) * conv_o + sgat_ref[0, rows, :].astype(F32) * attn_o
        mix = _mm(merged.astype(BF16), wo_ref[...])
        x1 = x_ref[0, rows, :] + mod_ref[0, :, 2 * d:3 * d] * mix
        x1_ref[0, rows, :] = x1
        h2 = (_rms(x1) * n2_ref[...]) * (1.0 + mod_ref[0, :, 4 * d:5 * d]) + mod_ref[0, :, 3 * d:4 * d]

        h_hi, h_lo = _split_bf16(h2)
        w_lo = lax.bitcast_convert_type(h_hi[:, :half].astype(F32), U32) >> 16
        w_hi = lax.bitcast_convert_type(h_hi[:, half:].astype(F32), U32) & jnp.uint32(0xFFFF0000)
        hx_ref[0, rows, 0:half] = w_lo | w_hi

        logits = (_mm(h_hi, r_hi) + _mm(h_hi, r_lo)
                  + _mm(h_lo, r_hi))
        logits = jnp.where(valid, logits, -jnp.inf)
        e = jnp.exp(logits - jnp.max(logits, axis=-1, keepdims=True))
        e = jnp.where(valid, e, 0.0)
        aff = e / jnp.sum(e, axis=-1, keepdims=True)
        hx_ref[0, rows, half:half + LANES] = lax.bitcast_convert_type(aff, U32)
        afft_ref[0, :, rows] = aff.T[0:N_EXPERTS, :]


def _postmix(attn, u, gb, sga, sgat, x, mod_l, row_of_batch, lw):
    b, n, d = x.shape
    tm = min(MIX_TILE, n)
    nsub = n // SUBLANES
    per = tm // SUBLANES
    tok_spec = lambda w: pl.BlockSpec((1, tm, w), lambda bi, i: (bi, i, 0))
    specs = [tok_spec(Q_W), tok_spec(CONV_W),
             pl.BlockSpec((1, SUBLANES, CONV_W), lambda bi, i: (bi, jnp.maximum(i * per - 1, 0), 0)),
             pl.BlockSpec((1, SUBLANES, CONV_W), lambda bi, i: (bi, jnp.minimum((i + 1) * per, nsub - 1), 0)),
             tok_spec(CONV_W), tok_spec(d), tok_spec(d), tok_spec(d),
             pl.BlockSpec((1, 1, 6 * d), lambda bi, i: (row_of_batch(bi), 0, 0)),
             _const_spec((1, d)), _const_spec((SUBLANES, CONV_W)),
             _const_spec((Q_W, d)), _const_spec((CONV_W, d)), _const_spec((d, d)), _const_spec((d, LANES))]
    outs = [jax.ShapeDtypeStruct((b, n, d), F32), jax.ShapeDtypeStruct((b, n, GATHER_W), U32),
            jax.ShapeDtypeStruct((b, N_EXPERTS, n), F32)]
    ospecs = [tok_spec(d), tok_spec(GATHER_W),
              pl.BlockSpec((1, N_EXPERTS, tm), lambda bi, i: (bi, 0, i))]
    return pl.pallas_call(
        functools.partial(_postmix_body, tm=tm),
        out_shape=outs, grid=(b, n // tm), in_specs=specs, out_specs=ospecs,
        compiler_params=_cparams(("arbitrary", "arbitrary")),
        name="postmix",
    )(attn, u, u, u, gb, sga, sgat, x, mod_l, lw["norm2"], lw["conv_w"], lw["w_attn_out"], lw["w_conv_out"],
      lw["w_o"], lw["w_router"])


def _topk_body(aff_ref, ut_ref, idx_ref, sel_ref, pos_ref, *, gb, n, cap):
    ne = N_EXPERTS
    rows_all = gb * ne
    a = aff_ref[...].reshape(rows_all, n)
    bits = lax.bitcast_convert_type(a, I32)

    def count(mask):
        return jnp.sum(jnp.where(mask, 1.0, 0.0), axis=1, keepdims=True)

    def thr_step(t, thr):
        cand = thr | (jnp.int32(1) << (30 - t))
        return jnp.where(count(bits >= cand) >= cap, cand, thr)

    thr = lax.fori_loop(0, 31, thr_step, jnp.zeros((rows_all, 1), I32))
    gt = bits > thr
    eq = bits == thr
    need = cap - count(gt)
    tok = lax.broadcasted_iota(I32, (rows_all, n), 1)
    nbits = int(np.log2(n)) + 1

    def tie_step(t, bound):
        cand = bound + (jnp.int32(1) << (nbits - 1 - t))
        ok = (cand <= n) & (count(eq & (tok < cand)) <= need)
        return jnp.where(ok, cand, bound)

    bound = lax.fori_loop(0, nbits, tie_step, jnp.zeros((rows_all, 1), I32))
    sel_ref[...] = jnp.where(gt | (eq & (tok < bound)), 1.0, 0.0)

    cb = min(CUMSUM_BLOCK, n)
    nblk = n // cb
    nrb = max(cap // RANK_BLOCK, 1)
    width = min(cap, RANK_BLOCK)
    tiles = RANK_BLOCK // SUBLANES
    lane = lax.broadcasted_iota(I32, (RANK_BLOCK, LANES), 1)
    rank0 = (lax.broadcasted_iota(I32, (tiles, SUBLANES, LANES), 0) * SUBLANES
             + lax.broadcasted_iota(I32, (tiles, SUBLANES, LANES), 1)).astype(F32)

    def batch_step(bi, carry):
        sel = sel_ref[pl.ds(pl.multiple_of(bi * ne, ne), ne), :]
        off = jnp.zeros((ne, 1), F32)
        for k in range(nblk):
            blk = _mm(sel[:, k * cb:(k + 1) * cb].astype(BF16), ut_ref[...])
            pos = blk + off
            for ex in range(ne):
                pos_ref[ex, :, k * cb:(k + 1) * cb] = jnp.broadcast_to(pos[ex:ex + 1, :], (SUBLANES, cb))
            off = off + blk[:, cb - 1:cb]

        def expert_step(ei, cols):
            for rb in range(nrb):
                rank = rank0 + float(rb * RANK_BLOCK)
                cnt = jnp.zeros((tiles, SUBLANES, LANES), F32)
                for lc in range(n // LANES):
                    prow = pos_ref[ei, :, lc * LANES:(lc + 1) * LANES]
                    cnt = cnt + jnp.where(prow[None] <= rank, 1.0, 0.0)
                col = jnp.sum(cnt.reshape(RANK_BLOCK, LANES), axis=1, keepdims=True)
                cols = jnp.where(lane == ei * nrb + rb, col, cols)
            return cols

        cols = lax.fori_loop(0, ne, expert_step, jnp.zeros((RANK_BLOCK, LANES), F32))
        rows = cols.T
        idx_ref[bi] = rows[0:ne * nrb, 0:width].astype(I32)
        return carry

    lax.fori_loop(0, gb, batch_step, 0)


def _topk(afft, consts):
    b, ne, n = afft.shape
    cap = max(1, EC_FACTOR * n // N_EXPERTS)
    nrb = max(cap // RANK_BLOCK, 1)
    width = min(cap, RANK_BLOCK)
    cb = min(CUMSUM_BLOCK, n)
    gb = min(b, max(1, TOPK_ROWS // (ne * n)))
    assert b % gb == 0
    idx = pl.pallas_call(
        functools.partial(_topk_body, gb=gb, n=n, cap=cap),
        out_shape=jax.ShapeDtypeStruct((b, ne * nrb, width), I32),
        grid=(b // gb,),
        in_specs=[pl.BlockSpec((gb, ne, n), lambda bi: (bi, 0, 0)), _const_spec((cb, cb))],
        out_specs=pl.BlockSpec((gb, ne * nrb, width), lambda bi: (bi, 0, 0)),
        scratch_shapes=[pltpu.VMEM((gb * ne, n), F32), pltpu.VMEM((ne, SUBLANES, n), F32)],
        compiler_params=_cparams(("arbitrary",)),
        name="topk",
    )(afft, consts["ut"][:cb, :cb])
    return idx.reshape(b, ne, cap)


def _ffn_body(idx_ref, idx_next_ref, hx_ref, wg_ref, wu_ref, wd_ref, y_ref, xs_ref, *, rows):
    e = pl.program_id(1)

    def gather_row(src_ref, slot, j, k):
        t = src_ref[0, 0, j * SUBLANES + k]
        xs_ref[slot, j, pl.ds(k, 1), :] = hx_ref[0, pl.ds(t, 1), :]

    @pl.when(e == 0)
    def _gather_first_expert():
        def body(j, carry):
            for k in range(SUBLANES):
                gather_row(idx_ref, 0, j, k)
            return carry
        lax.fori_loop(0, rows // SUBLANES, body, 0)

    slot = lax.rem(e, 2)
    half = D_MODEL // 2
    gathered = xs_ref[slot].reshape(rows, GATHER_W)
    for j in range(rows // SUBLANES):
        for k in range(SUBLANES):
            gather_row(idx_next_ref, 1 - slot, j, k)
    words = gathered[:, 0:half]
    aff = lax.bitcast_convert_type(gathered[:, half:half + LANES], F32)
    x_lo = lax.bitcast_convert_type(words << 16, F32).astype(BF16)
    x_hi = lax.bitcast_convert_type(words & jnp.uint32(0xFFFF0000), F32).astype(BF16)

    def up_proj(w_ref):
        return (_mm(x_lo, w_ref[0, 0, 0:half, :])
                + _mm(x_hi, w_ref[0, 0, half:, :]))

    gate = up_proj(wg_ref)
    hidden = ((gate * jax.nn.sigmoid(gate)) * up_proj(wu_ref)).astype(BF16)
    y = _mm(hidden, wd_ref[0, 0])
    lane = lax.broadcasted_iota(I32, (rows, LANES), 1)
    val = jnp.sum(jnp.where(lane == e, aff, 0.0), axis=1, keepdims=True)
    y_ref[0, 0] = (y * val).reshape(rows // SUBLANES, SUBLANES, D_MODEL)


def _moe_ffn(idx_steps, hx, lw, group):
    ng, ne, rows = idx_steps.shape
    gtok = hx.shape[1] * group
    hx = hx.reshape(ng, gtok, GATHER_W)
    idx_flat = idx_steps.reshape(ng * ne, 1, rows)
    layer = lw["layer"]
    w_spec = lambda r, c: pl.BlockSpec((1, 1, r, c), lambda g, e: (layer, e, 0, 0))
    idx_spec = lambda step: pl.BlockSpec((1, 1, rows), lambda g, e: (g * ne + jnp.minimum(e + step, ne - 1), 0, 0),
                                         memory_space=pltpu.SMEM)
    return pl.pallas_call(
        functools.partial(_ffn_body, rows=rows),
        out_shape=jax.ShapeDtypeStruct((ng, ne, rows // SUBLANES, SUBLANES, D_MODEL), F32),
        grid=(ng, ne),
        in_specs=[idx_spec(0), idx_spec(1), pl.BlockSpec((1, gtok, GATHER_W), lambda g, e: (g, 0, 0)),
                  w_spec(D_MODEL, D_EXPERT), w_spec(D_MODEL, D_EXPERT), w_spec(D_EXPERT, D_MODEL)],
        out_specs=pl.BlockSpec((1, 1, rows // SUBLANES, SUBLANES, D_MODEL), lambda g, e: (g, e, 0, 0, 0)),
        scratch_shapes=[pltpu.VMEM((2, rows // SUBLANES, SUBLANES, GATHER_W), U32)],
        compiler_params=_cparams(("arbitrary", "arbitrary")),
        name="moe_ffn",
    )(idx_flat, idx_flat, hx, lw["w_gate"], lw["w_up"], lw["w_down"])


def _combine_body(idx_ref, y_ref, o_ref, *, rows):
    @pl.when(pl.program_id(1) == 0)
    def _zero():
        o_ref[...] = jnp.zeros(o_ref.shape, F32)

    def scatter(j, carry):
        toks = [idx_ref[0, 0, j * SUBLANES + k] for k in range(SUBLANES)]
        vals = [o_ref[0, pl.ds(toks[k], 1), :] + y_ref[0, 0, j, pl.ds(k, 1), :] for k in range(SUBLANES)]
        for k in range(SUBLANES):
            o_ref[0, pl.ds(toks[k], 1), :] = vals[k]
        return carry

    lax.fori_loop(0, rows // SUBLANES, scatter, 0)


def _combine(idx_steps, ysel, gtok):
    ng, ne, rows = idx_steps.shape
    return pl.pallas_call(
        functools.partial(_combine_body, rows=rows),
        out_shape=jax.ShapeDtypeStruct((ng, gtok, D_MODEL), F32),
        grid=(ng, ne),
        in_specs=[pl.BlockSpec((1, 1, rows), lambda g, e: (g * ne + e, 0, 0), memory_space=pltpu.SMEM),
                  pl.BlockSpec((1, 1, rows // SUBLANES, SUBLANES, D_MODEL), lambda g, e: (g, e, 0, 0, 0))],
        out_specs=pl.BlockSpec((1, gtok, D_MODEL), lambda g, e: (g, 0, 0)),
        compiler_params=_cparams(("arbitrary", "arbitrary")),
        name="moe_combine",
    )(idx_steps.reshape(ng * ne, 1, rows), ysel)


def _expert_choice(hx, afft, lw, consts, group):
    b, n, _ = hx.shape
    idx = _topk(afft, consts)
    cap = idx.shape[-1]
    ng = b // group
    offs = (jnp.arange(b, dtype=I32) % group * n).reshape(ng, group, 1, 1)
    idx_steps = (idx.reshape(ng, group, N_EXPERTS, cap) + offs).transpose(0, 2, 1, 3).reshape(ng, N_EXPERTS, group * cap)
    ysel = _moe_ffn(idx_steps, hx, lw, group)
    moe = _combine(idx_steps, ysel, group * n)
    return moe.reshape(b, n, D_MODEL)


def _final_body(x1_ref, moe_ref, mod_ref, g_ref, o_ref):
    d = D_MODEL
    x = x1_ref[0] + mod_ref[0, :, 5 * d:6 * d] * moe_ref[0]
    o_ref[0] = _rms(x) * g_ref[...]


def _final_norm(x1, moe, mod_l, row_of_batch, g):
    b, n, d = x1.shape
    tm = min(FINAL_TILE, n)
    tok = pl.BlockSpec((1, tm, d), lambda bi, i: (bi, i, 0))
    return pl.pallas_call(
        _final_body,
        out_shape=jax.ShapeDtypeStruct((b, n, d), F32),
        grid=(b, n // tm),
        in_specs=[tok, tok, pl.BlockSpec((1, 1, 6 * d), lambda bi, i: (row_of_batch(bi), 0, 0)), _const_spec((1, d))],
        out_specs=tok,
        compiler_params=_cparams(("arbitrary", "arbitrary")),
        name="final_norm",
    )(x1, moe, mod_l, g)


def _rope_tables(n_lat):
    half = AXIS_DIM // 2
    inv_freq = ROPE_THETA ** (-jnp.arange(half, dtype=F32) / half)
    t = jnp.arange(n_lat, dtype=jnp.int32)
    row = (t // GRID_W).astype(F32)
    col = (t % GRID_W).astype(F32)
    lane = np.arange(LANES)
    dim = lane % HEAD_DIM
    use_col = jnp.asarray(dim >= AXIS_DIM)
    freq = inv_freq[jnp.asarray(dim % half)]
    pos = jnp.where(use_col[None, :], col[:, None], row[:, None])
    ang = pos * freq[None, :]
    sign = jnp.asarray(np.where(dim % AXIS_DIM < half, -1.0, 1.0), dtype=F32)
    return jnp.cos(ang), jnp.sin(ang) * sign[None, :]


def _block_diag_ones(width):
    seg = np.arange(width) // HEAD_DIM
    return jnp.asarray(seg[:, None] == seg[None, :], dtype=BF16)


def kernel(x_prompt, x_sample, cache_k, cache_v, c, c_ctx, w_mod, b_mod, norm1, norm2, w_in, q_norm, k_norm,
           conv_w, w_conv_out, w_attn_out, w_o, w_router, w_gate, w_up, w_down, final_norm):
    depth = w_mod.shape[0]
    bc, nc, d = x_prompt.shape
    bl, nl, _ = x_sample.shape
    p_len = cache_k.shape[2]
    assert d == D_MODEL and bl + 1 <= MOD_ROWS
    assert nc % min(ROW_CHAIN, nc) == 0 and nl % KEY_BLOCK == 0 and p_len % LANES == 0

    cmat = jnp.zeros((MOD_ROWS, d), F32).at[0].set(c_ctx).at[1:1 + bl].set(c)
    mods = _modulation(cmat, w_mod, b_mod).reshape(depth, MOD_ROWS, 1, 6 * d)

    cos_t, sin_t = _rope_tables(nl)
    ut = np.arange(CUMSUM_BLOCK)
    consts = {"bdq": _block_diag_ones(Q_W), "bdk": _block_diag_ones(KV_W), "cos": cos_t, "sin": sin_t,
              "ut": jnp.asarray(ut[:, None] <= ut[None, :], dtype=BF16)}

    kc_rep = jnp.tile(cache_k.transpose(0, 1, 3, 2, 4), (1, 1, 1, 1, GROUP)).astype(BF16)
    vc_t = cache_v.transpose(0, 1, 3, 4, 2).reshape(bl, depth, KV_W, p_len).astype(BF16)

    ctx_row = lambda bi: 0
    lat_row = lambda bi: bi + 1
    ctx_group = min(CTX_GROUP, bc)

    w_gate_b, w_up_b, w_down_b = w_gate.astype(BF16), w_up.astype(BF16), w_down.astype(BF16)
    state = {"ctx": (x_prompt, None), "lat": (x_sample, None)}
    new_k, new_v = [], []
    for l in range(depth):
        lw = {
            "norm1": norm1[l].reshape(1, d), "norm2": norm2[l].reshape(1, d),
            "w_in": w_in[l].astype(BF16),
            "gq": jnp.tile(q_norm[l], N_HEADS).reshape(1, Q_W), "gk": jnp.tile(k_norm[l], N_KV_HEADS).reshape(1, KV_W),
            "conv_w": jnp.zeros((SUBLANES, CONV_W), F32).at[0:3].set(conv_w[l]),
            "w_conv_out": w_conv_out[l].astype(BF16), "w_attn_out": w_attn_out[l].astype(BF16),
            "w_o": w_o[l].astype(BF16),
            "w_router": jnp.zeros((d, LANES), F32).at[:, 0:N_EXPERTS].set(w_router[l]),
            "w_gate": w_gate_b, "w_up": w_up_b, "w_down": w_down_b, "layer": l,
        }
        mod_l = mods[l]
        modp = mods[l - 1] if l else None
        for name in ("ctx", "lat"):
            is_lat = name == "lat"
            row_fn = lat_row if is_lat else ctx_row
            xa, xb = state[name]
            res = _premix(xa, xb, modp if xb is not None else None, mod_l, row_fn, lw, consts, is_lat)
            q, krep, vt = res[0:3]
            pos = 3
            if not is_lat:
                new_k.append(res[3])
                new_v.append(res[4])
                pos = 5
            u, gb, sga, sgat = res[pos:pos + 4]
            x_cur = res[pos + 4] if xb is not None else xa
            if is_lat:
                attn = _attention(q, krep, vt, kc_rep[:, l], vc_t[:, l])
            else:
                attn = _attention(q, krep, vt)
            x1, hx, afft = _postmix(attn, u, gb, sga, sgat, x_cur, mod_l, row_fn, lw)
            moe = _expert_choice(hx, afft, lw, consts, group=1 if is_lat else ctx_group)
            state[name] = (x1, moe)

    g = final_norm.reshape(1, d)
    y_prompt = _final_norm(*state["ctx"], mods[depth - 1], ctx_row, g)
    y_sample = _final_norm(*state["lat"], mods[depth - 1], lat_row, g)
    shape_kv = (bc, depth, nc, N_KV_HEADS, HEAD_DIM)
    new_cache_k = jnp.stack(new_k, axis=1).reshape(shape_kv)
    new_cache_v = jnp.stack(new_v, axis=1).reshape(shape_kv)
    return (y_prompt, y_sample, new_cache_k, new_cache_v)
```

```python
import functools

import jax
import jax.numpy as jnp
import numpy as np
from jax import lax
from jax.experimental import pallas as pl
from jax.experimental.pallas import tpu as pltpu

F32 = jnp.float32
BF16 = jnp.bfloat16
I32 = jnp.int32
U32 = jnp.uint32

D_MODEL = 1024
N_HEADS = 16
N_KV_HEADS = 4
HEAD_DIM = 64
GROUP = N_HEADS // N_KV_HEADS
Q_W = N_HEADS * HEAD_DIM
KV_W = N_KV_HEADS * HEAD_DIM
CONV_W = D_MODEL // 2
IN_W = Q_W + 2 * KV_W + 3 * CONV_W + 2 * D_MODEL
N_EXPERTS = 16
EC_FACTOR = 2
D_EXPERT = 1024
GRID_W = 64
AXIS_DIM = HEAD_DIM // 2
ROPE_THETA = 10000.0
EPS = 1e-6
MOD_ROWS = 16

LANES = 128
SUBLANES = 8
VMEM_LIMIT_BYTES = 56 * 1024 * 1024

MIX_TILE = 512
ROW_CHAIN = 256
FINAL_TILE = 1024
KEY_BLOCK = 1024
QUERY_TILE = 512
CUMSUM_BLOCK = 256
RANK_BLOCK = 128
TOPK_ROWS = 512 * 1024
CTX_GROUP = 16
ONES_ROWS = 16
GATHER_W = D_MODEL // 2 + LANES
LOG2_E = 1.4426950408889634


def _cparams(sem):
    return pltpu.CompilerParams(dimension_semantics=sem, vmem_limit_bytes=VMEM_LIMIT_BYTES)


def _mm(a, b):
    return jnp.dot(a, b, preferred_element_type=F32)


def _const_spec(shape, single=False):
    nd = len(shape)
    return pl.BlockSpec(shape, lambda *_: (0,) * nd, pipeline_mode=pl.Buffered(1) if single else None)


def _mod_body(c_ref, w_ref, b_ref, o_ref):
    c = c_ref[...]
    s = c * jax.nn.sigmoid(c)
    o_ref[0] = _mm(s.astype(BF16), w_ref[0].astype(BF16)) + b_ref[0]


def _modulation(cmat, w_mod, b_mod):
    depth = w_mod.shape[0]
    tn = 1536
    return pl.pallas_call(
        _mod_body,
        out_shape=jax.ShapeDtypeStruct((depth, MOD_ROWS, 6 * D_MODEL), F32),
        grid=(depth, 6 * D_MODEL // tn),
        in_specs=[
            pl.BlockSpec((MOD_ROWS, D_MODEL), lambda l, j: (0, 0)),
            pl.BlockSpec((1, D_MODEL, tn), lambda l, j: (l, 0, j)),
            pl.BlockSpec((1, 1, tn), lambda l, j: (l, 0, j)),
        ],
        out_specs=pl.BlockSpec((1, MOD_ROWS, tn), lambda l, j: (l, 0, j)),
        compiler_params=_cparams(("arbitrary", "arbitrary")),
        name="modulation",
    )(cmat, w_mod, b_mod.reshape(depth, 1, 6 * D_MODEL))


def _rms(x):
    return x * lax.rsqrt(jnp.mean(x * x, axis=-1, keepdims=True) + EPS)


def _premix_body(*refs, is_lat, has_prev, tm, n_cast):
    it = iter(refs)
    xa_ref = next(it)
    if has_prev:
        xb_ref = next(it)
        modp_ref = next(it)
    mod_ref = next(it)
    n1_ref = next(it)
    win_ref = next(it)
    gq_ref = next(it)
    gk_ref = next(it)
    bdq_ref = next(it)
    bdk_ref = next(it)
    if is_lat:
        cos_ref = next(it)
        sin_ref = next(it)
    cast_in = [next(it) for _ in range(n_cast)]
    cast_out = [next(it) for _ in range(n_cast)]
    for src, dst in zip(cast_in, cast_out):
        dst[...] = src[0].astype(BF16)
    q_ref = next(it)
    krep_ref = next(it)
    vt_ref = next(it)
    if not is_lat:
        kc_ref = next(it)
        vc_ref = next(it)
    u_ref = next(it)
    gb_ref = next(it)
    sga_ref = next(it)
    sgat_ref = next(it)
    if has_prev:
        xn_ref = next(it)

    d = D_MODEL
    sub = min(ROW_CHAIN, tm)
    lane = lax.broadcasted_iota(I32, (sub, LANES), 1)
    first_half = (lane % AXIS_DIM) < (AXIS_DIM // 2)
    low_head = lane < HEAD_DIM
    sh1 = mod_ref[0, :, 0:d]
    sc1 = mod_ref[0, :, d:2 * d]

    def row_chain(rows):
        x = xa_ref[0, rows, :]
        if has_prev:
            x = x + modp_ref[0, :, 5 * d:6 * d] * xb_ref[0, rows, :]
            xn_ref[0, rows, :] = x
        h = (_rms(x) * n1_ref[...]) * (1.0 + sc1) + sh1
        hb = h.astype(BF16)

        def proj(lo, hi):
            return _mm(hb, win_ref[:, lo:hi])

        def rope(chunk):
            if not is_lat:
                return chunk
            partner = jnp.where(first_half, pltpu.roll(chunk, LANES - AXIS_DIM // 2, 1),
                                pltpu.roll(chunk, AXIS_DIM // 2, 1))
            return chunk * cos_ref[rows, :] + partner * sin_ref[rows, :]

        pq = proj(0, Q_W)
        ssq = _mm((pq * pq).astype(BF16), bdq_ref[...])
        qn = pq * lax.rsqrt(ssq * (1.0 / HEAD_DIM) + EPS) * gq_ref[...]
        for c in range(Q_W // LANES):
            qc = rope(qn[:, c * LANES:(c + 1) * LANES])
            q_ref[0, rows, c * LANES:(c + 1) * LANES] = (qc * (HEAD_DIM ** -0.5 * LOG2_E)).astype(BF16)

        pk = proj(Q_W, Q_W + KV_W)
        ssk = _mm((pk * pk).astype(BF16), bdk_ref[...])
        kn = pk * lax.rsqrt(ssk * (1.0 / HEAD_DIM) + EPS) * gk_ref[...]
        if not is_lat:
            kc_ref[0, rows, :] = kn
        for c in range(KV_W // LANES):
            kc = rope(kn[:, c * LANES:(c + 1) * LANES])
            rolled = pltpu.roll(kc, HEAD_DIM, 1)
            even = jnp.where(low_head, kc, rolled).astype(BF16)
            odd = jnp.where(low_head, rolled, kc).astype(BF16)
            for s in range(KV_W // LANES):
                krep_ref[0, 2 * c, rows, s * LANES:(s + 1) * LANES] = even
                krep_ref[0, 2 * c + 1, rows, s * LANES:(s + 1) * LANES] = odd

        pv = proj(Q_W + KV_W, Q_W + 2 * KV_W)
        if not is_lat:
            vc_ref[0, rows, :] = pv
        vt_ref[0, 0, :, rows] = pv.T.astype(BF16)

        o = Q_W + 2 * KV_W
        gb_ref[0, rows, :] = proj(o, o + CONV_W).astype(BF16)
        u_ref[0, rows, :] = proj(o + CONV_W, o + 2 * CONV_W) * proj(o + 2 * CONV_W, o + 3 * CONV_W)
        o = o + 3 * CONV_W
        sga_ref[0, rows, :] = jax.nn.sigmoid(proj(o, o + d)).astype(BF16)
        sgat_ref[0, rows, :] = jax.nn.sigmoid(proj(o + d, o + 2 * d)).astype(BF16)

    for c in range(tm // sub):
        row_chain(slice(c * sub, (c + 1) * sub))


def _premix(xa, xb, modp, mod_l, row_of_batch, lw, consts, is_lat, cast=()):
    b, n, d = xa.shape
    tm = min(MIX_TILE, n)
    tk = min(KEY_BLOCK, n)
    has_prev = xb is not None
    sub = tk // tm
    steps = n // tm
    tok_spec = lambda w: pl.BlockSpec((1, tm, w), lambda bi, i: (bi, i, 0))
    mod_spec = pl.BlockSpec((1, 1, 6 * d), lambda bi, i: (row_of_batch(bi), 0, 0))

    ins, specs = [xa], [tok_spec(d)]
    if has_prev:
        ins += [xb, modp]
        specs += [tok_spec(d), mod_spec]
    ins += [mod_l, lw["norm1"], lw["w_in"], lw["gq"], lw["gk"], consts["bdq"], consts["bdk"]]
    specs += [mod_spec, _const_spec((1, d)), _const_spec((d, IN_W), single=True), _const_spec((1, Q_W)),
              _const_spec((1, KV_W)), _const_spec((Q_W, Q_W), single=True), _const_spec((KV_W, KV_W))]
    if is_lat:
        ins += [consts["cos"], consts["sin"]]
        specs += [pl.BlockSpec((tm, LANES), lambda bi, i: (i, 0))] * 2

    outs, ospecs = [], []
    for w, layer in cast:
        _, wrows, wcols = w.shape
        slab = wrows // (b * steps)
        assert slab * b * steps == wrows and slab % SUBLANES == 0
        ins.append(w)
        specs.append(pl.BlockSpec((1, slab, wcols), lambda bi, i, layer=layer: (layer, bi * steps + i, 0)))
        outs.append(jax.ShapeDtypeStruct((wrows, wcols), BF16))
        ospecs.append(pl.BlockSpec((slab, wcols), lambda bi, i: (bi * steps + i, 0)))
    outs += [jax.ShapeDtypeStruct((b, n, Q_W), BF16),
             jax.ShapeDtypeStruct((b, N_KV_HEADS, n, KV_W), BF16),
             jax.ShapeDtypeStruct((b, n // tk, KV_W, tk), BF16)]
    ospecs += [tok_spec(Q_W),
               pl.BlockSpec((1, N_KV_HEADS, tm, KV_W), lambda bi, i: (bi, 0, i, 0)),
               pl.BlockSpec((1, 1, KV_W, tm), lambda bi, i: (bi, i // sub, 0, i % sub))]
    if not is_lat:
        outs += [jax.ShapeDtypeStruct((b, n, KV_W), F32)] * 2
        ospecs += [tok_spec(KV_W)] * 2
    outs += [jax.ShapeDtypeStruct((b, n, CONV_W), F32), jax.ShapeDtypeStruct((b, n, CONV_W), BF16),
             jax.ShapeDtypeStruct((b, n, d), BF16), jax.ShapeDtypeStruct((b, n, d), BF16)]
    ospecs += [tok_spec(CONV_W), tok_spec(CONV_W), tok_spec(d), tok_spec(d)]
    if has_prev:
        outs.append(jax.ShapeDtypeStruct((b, n, d), F32))
        ospecs.append(tok_spec(d))

    res = pl.pallas_call(
        functools.partial(_premix_body, is_lat=is_lat, has_prev=has_prev, tm=tm, n_cast=len(cast)),
        out_shape=outs, grid=(b, n // tm), in_specs=specs, out_specs=ospecs,
        compiler_params=_cparams(("arbitrary", "arbitrary")),
        name="premix_lat" if is_lat else "premix_ctx",
    )(*ins)
    return list(res)


def _attn_body(*refs, p_len, n, tq, tk, hps):
    it = iter(refs)
    q_ref = next(it)
    if p_len:
        kc_ref = next(it)
        vc_ref = next(it)
    k_ref = next(it)
    v_ref = next(it)
    o_ref = next(it)
    qbd_ref = next(it)
    s_ref = next(it)
    m_ref = next(it)
    acc_ref = next(it)

    m_ref[...] = jnp.full(m_ref.shape, -jnp.inf, F32)
    acc_ref[...] = jnp.zeros(acc_ref.shape, F32)
    nb = n // tk

    def one_kv_head(hh):
        base = hh * GROUP
        vrows = slice(hh * HEAD_DIM, (hh + 1) * HEAD_DIM)
        lanes = slice(hh * KV_W, (hh + 1) * KV_W)

        qt = q_ref[0, :, lanes].astype(F32).T
        head_of_row = lax.broadcasted_iota(I32, (KV_W, tq), 0) // HEAD_DIM
        for g in range(GROUP):
            qbd_ref[base + g] = jnp.where(head_of_row == g, qt, 0.0).astype(BF16)

        def score(parts, g):
            off, m_blk = 0, None
            for get_k, _, size in parts:
                s = _mm(get_k(), qbd_ref[base + g])
                s_ref[base + g, off:off + size, :] = s
                m_part = jnp.max(s, axis=0, keepdims=True)
                m_blk = m_part if m_blk is None else jnp.maximum(m_blk, m_part)
                off += size
            return m_blk

        def softmax_pv(parts, g, m_blk):
            size = sum(part[2] for part in parts)
            m_prev = m_ref[base + g]
            m_new = jnp.maximum(m_prev, m_blk)
            alpha = jnp.exp2(m_prev - m_new)
            p = jnp.exp2(s_ref[base + g, 0:size, :] - m_new).astype(BF16)
            v_ext = jnp.concatenate([jnp.concatenate([get_v() for _, get_v, _ in parts], axis=1),
                                     jnp.ones((ONES_ROWS, size), BF16)], axis=0)
            acc_ref[base + g] = alpha * acc_ref[base + g] + _mm(v_ext, p)
            m_ref[base + g] = m_new

        def key_block(parts, m_blk, next_parts):
            for g in range(GROUP):
                if g + 1 < GROUP:
                    m_next = score(parts, g + 1)
                elif next_parts is not None:
                    m_next = score(next_parts, 0)
                else:
                    m_next = None
                softmax_pv(parts, g, m_blk)
                m_blk = m_next
            return m_blk

        def block_parts(i):
            if isinstance(i, int):
                parts = [(lambda: k_ref[0, hh, i * tk:(i + 1) * tk, :], lambda: v_ref[0, i, vrows, :], tk)]
                if p_len and i == 0:
                    parts.append((lambda: kc_ref[0, hh], lambda: vc_ref[0, vrows, :], p_len))
                return parts
            return [(lambda: k_ref[0, hh, pl.ds(pl.multiple_of(i * tk, tk), tk), :], lambda: v_ref[0, i, vrows, :], tk)]

        m_blk = score(block_parts(0), 0)
        if nb > 1:
            m_blk = key_block(block_parts(0), m_blk, block_parts(1))
            m_blk = lax.fori_loop(1, nb - 1, lambda i, m: key_block(block_parts(i), m, block_parts(i + 1)), m_blk)
        key_block(block_parts(nb - 1), m_blk, None)

        outs = []
        for g in range(GROUP):
            acc = acc_ref[base + g]
            outs.append(acc[0:HEAD_DIM, :] * (1.0 / acc[HEAD_DIM:HEAD_DIM + 1, :]))
        o_ref[0, :, lanes] = jnp.concatenate(outs, axis=0).T.astype(BF16)

    for hh in range(hps):
        one_kv_head(hh)


def _attention(q, krep, vt, kc_rep=None, vc_t=None):
    b, n, _ = q.shape
    tk = vt.shape[-1]
    tq = min(QUERY_TILE, n)
    p_len = 0 if kc_rep is None else kc_rep.shape[2]
    hps = N_KV_HEADS if n == tk else 1
    ins, specs = [q], [pl.BlockSpec((1, tq, hps * KV_W), lambda bi, h, i: (bi, i, h))]
    if p_len:
        ins += [kc_rep, vc_t]
        specs += [pl.BlockSpec((1, hps, p_len, KV_W), lambda bi, h, i: (bi, h, 0, 0)),
                  pl.BlockSpec((1, hps * HEAD_DIM, p_len), lambda bi, h, i: (bi, h, 0))]
    ins += [krep, vt]
    specs += [pl.BlockSpec((1, hps, n, KV_W), lambda bi, h, i: (bi, h, 0, 0)),
              pl.BlockSpec((1, n // tk, hps * HEAD_DIM, tk), lambda bi, h, i: (bi, 0, h, 0))]
    return pl.pallas_call(
        functools.partial(_attn_body, p_len=p_len, n=n, tq=tq, tk=tk, hps=hps),
        out_shape=jax.ShapeDtypeStruct((b, n, Q_W), BF16),
        grid=(b, N_KV_HEADS // hps, n // tq),
        in_specs=specs,
        out_specs=pl.BlockSpec((1, tq, hps * KV_W), lambda bi, h, i: (bi, i, h)),
        scratch_shapes=[pltpu.VMEM((hps * GROUP, KV_W, tq), BF16),
                        pltpu.VMEM((hps * GROUP, tk + p_len, tq), F32),
                        pltpu.VMEM((hps * GROUP, 1, tq), F32),
                        pltpu.VMEM((hps * GROUP, HEAD_DIM + ONES_ROWS, tq), F32)],
        compiler_params=_cparams(("arbitrary", "arbitrary", "arbitrary")),
        name="attention_lat" if p_len else "attention_ctx",
    )(*ins)


def _split_bf16(x):
    hi = x.astype(BF16)
    lo = (x - hi.astype(F32)).astype(BF16)
    return hi, lo


def _postmix_body(attn_ref, u_ref, up_ref, un_ref, gb_ref, sga_ref, sgat_ref, x_ref, mod_ref, n2_ref, cw_ref,
                  wa_ref, wc_ref, wo_ref, wr_ref, x1_ref, hx_ref, afft_ref, *, tm):
    d = D_MODEL
    i = pl.program_id(1)
    last = pl.num_programs(1) - 1
    u = u_ref[0]
    row = lax.broadcasted_iota(I32, (tm, CONV_W), 0)
    prev_row = jnp.where(i > 0, up_ref[0, SUBLANES - 1:SUBLANES, :], 0.0)
    next_row = jnp.where(i < last, un_ref[0, 0:1, :], 0.0)
    u_m1 = jnp.where(row == 0, prev_row, pltpu.roll(u, 1, 0))
    u_p1 = jnp.where(row == tm - 1, next_row, pltpu.roll(u, tm - 1, 0))
    conv = cw_ref[0:1, :] * u_m1 + cw_ref[1:2, :] * u + cw_ref[2:3, :] * u_p1
    cv = (gb_ref[0].astype(F32) * conv).astype(BF16)
    r_hi, r_lo = _split_bf16(wr_ref[...])
    half = d // 2
    sub = min(ROW_CHAIN, tm)
    lane = lax.broadcasted_iota(I32, (sub, LANES), 1)
    valid = lane < N_EXPERTS

    for c in range(tm // sub):
        rows = slice(c * sub, (c + 1) * sub)
        conv_o = _mm(cv[rows], wc_ref[...])
        attn_o = _mm(attn_ref[0, rows, :], wa_ref[...])
        merged = sga_ref[0, rows, :].astype(F32) * conv_o + sgat_ref[0, rows, :].astype(F32) * attn_o
        mix = _mm(merged.astype(BF16), wo_ref[...])
        x1 = x_ref[0, rows, :] + mod_ref[0, :, 2 * d:3 * d] * mix
        x1_ref[0, rows, :] = x1
        h2 = (_rms(x1) * n2_ref[...]) * (1.0 + mod_ref[0, :, 4 * d:5 * d]) + mod_ref[0, :, 3 * d:4 * d]

        h_hi, h_lo = _split_bf16(h2)
        w_lo = lax.bitcast_convert_type(h_hi[:, :half].astype(F32), U32) >> 16
        w_hi = lax.bitcast_convert_type(h_hi[:, half:].astype(F32), U32) & jnp.uint32(0xFFFF0000)
        hx_ref[0, rows, 0:half] = w_lo | w_hi

        logits = (_mm(h_hi, r_hi) + _mm(h_hi, r_lo)
                  + _mm(h_lo, r_hi))
        logits = jnp.where(valid, logits, -jnp.inf)
        e = jnp.exp(logits - jnp.max(logits, axis=-1, keepdims=True))
        e = jnp.where(valid, e, 0.0)
        aff = e / jnp.sum(e, axis=-1, keepdims=True)
        hx_ref[0, rows, half:half + LANES] = lax.bitcast_convert_type(aff, U32)
        afft_ref[0, :, rows] = aff.T[0:N_EXPERTS, :]


def _postmix(attn, u, gb, sga, sgat, x, mod_l, row_of_batch, lw):
    b, n, d = x.shape
    tm = min(MIX_TILE, n)
    nsub = n // SUBLANES
    per = tm // SUBLANES
    tok_spec = lambda w: pl.BlockSpec((1, tm, w), lambda bi, i: (bi, i, 0))
    specs = [tok_spec(Q_W), tok_spec(CONV_W),
             pl.BlockSpec((1, SUBLANES, CONV_W), lambda bi, i: (bi, jnp.maximum(i * per - 1, 0), 0)),
             pl.BlockSpec((1, SUBLANES, CONV_W), lambda bi, i: (bi, jnp.minimum((i + 1) * per, nsub - 1), 0)),
             tok_spec(CONV_W), tok_spec(d), tok_spec(d), tok_spec(d),
             pl.BlockSpec((1, 1, 6 * d), lambda bi, i: (row_of_batch(bi), 0, 0)),
             _const_spec((1, d)), _const_spec((SUBLANES, CONV_W)),
             _const_spec((Q_W, d)), _const_spec((CONV_W, d)), _const_spec((d, d)), _const_spec((d, LANES))]
    outs = [jax.ShapeDtypeStruct((b, n, d), F32), jax.ShapeDtypeStruct((b, n, GATHER_W), U32),
            jax.ShapeDtypeStruct((b, N_EXPERTS, n), F32)]
    ospecs = [tok_spec(d), tok_spec(GATHER_W),
              pl.BlockSpec((1, N_EXPERTS, tm), lambda bi, i: (bi, 0, i))]
    return pl.pallas_call(
        functools.partial(_postmix_body, tm=tm),
        out_shape=outs, grid=(b, n // tm), in_specs=specs, out_specs=ospecs,
        compiler_params=_cparams(("arbitrary", "arbitrary")),
        name="postmix",
    )(attn, u, u, u, gb, sga, sgat, x, mod_l, lw["norm2"], lw["conv_w"], lw["w_attn_out"], lw["w_conv_out"],
      lw["w_o"], lw["w_router"])


def _topk_body(aff_ref, ut_ref, idx_ref, sel_ref, pos_ref, *, gb, n, cap):
    ne = N_EXPERTS
    rows_all = gb * ne
    a = aff_ref[...].reshape(rows_all, n)
    bits = lax.bitcast_convert_type(a, I32)

    def count(mask):
        return jnp.sum(jnp.where(mask, 1.0, 0.0), axis=1, keepdims=True)

    def thr_step(t, thr):
        cand = thr | (jnp.int32(1) << (30 - t))
        return jnp.where(count(bits >= cand) >= cap, cand, thr)

    thr = lax.fori_loop(0, 31, thr_step, jnp.zeros((rows_all, 1), I32))
    gt = bits > thr
    eq = bits == thr
    need = cap - count(gt)
    tok = lax.broadcasted_iota(I32, (rows_all, n), 1)
    nbits = int(np.log2(n)) + 1

    def tie_step(t, bound):
        cand = bound + (jnp.int32(1) << (nbits - 1 - t))
        ok = (cand <= n) & (count(eq & (tok < cand)) <= need)
        return jnp.where(ok, cand, bound)

    bound = lax.fori_loop(0, nbits, tie_step, jnp.zeros((rows_all, 1), I32))
    sel_ref[...] = jnp.where(gt | (eq & (tok < bound)), 1.0, 0.0)

    cb = min(CUMSUM_BLOCK, n)
    nblk = n // cb
    nrb = max(cap // RANK_BLOCK, 1)
    width = min(cap, RANK_BLOCK)
    tiles = RANK_BLOCK // SUBLANES
    lane = lax.broadcasted_iota(I32, (RANK_BLOCK, LANES), 1)
    rank0 = (lax.broadcasted_iota(I32, (tiles, SUBLANES, LANES), 0) * SUBLANES
             + lax.broadcasted_iota(I32, (tiles, SUBLANES, LANES), 1)).astype(F32)

    def batch_step(bi, carry):
        sel = sel_ref[pl.ds(pl.multiple_of(bi * ne, ne), ne), :]
        off = jnp.zeros((ne, 1), F32)
        for k in range(nblk):
            blk = _mm(sel[:, k * cb:(k + 1) * cb].astype(BF16), ut_ref[...])
            pos = blk + off
            for ex in range(ne):
                pos_ref[ex, :, k * cb:(k + 1) * cb] = jnp.broadcast_to(pos[ex:ex + 1, :], (SUBLANES, cb))
            off = off + blk[:, cb - 1:cb]

        def expert_step(ei, cols):
            for rb in range(nrb):
                rank = rank0 + float(rb * RANK_BLOCK)
                cnt = jnp.zeros((tiles, SUBLANES, LANES), F32)
                for lc in range(n // LANES):
                    prow = pos_ref[ei, :, lc * LANES:(lc + 1) * LANES]
                    cnt = cnt + jnp.where(prow[None] <= rank, 1.0, 0.0)
                col = jnp.sum(cnt.reshape(RANK_BLOCK, LANES), axis=1, keepdims=True)
                cols = jnp.where(lane == ei * nrb + rb, col, cols)
            return cols

        cols = lax.fori_loop(0, ne, expert_step, jnp.zeros((RANK_BLOCK, LANES), F32))
        rows = cols.T
        idx_ref[bi] = rows[0:ne * nrb, 0:width].astype(I32)
        return carry

    lax.fori_loop(0, gb, batch_step, 0)


def _topk(afft, consts):
    b, ne, n = afft.shape
    cap = max(1, EC_FACTOR * n // N_EXPERTS)
    nrb = max(cap // RANK_BLOCK, 1)
    width = min(cap, RANK_BLOCK)
    cb = min(CUMSUM_BLOCK, n)
    gb = min(b, max(1, TOPK_ROWS // (ne * n)))
    assert b % gb == 0
    idx = pl.pallas_call(
        functools.partial(_topk_body, gb=gb, n=n, cap=cap),
        out_shape=jax.ShapeDtypeStruct((b, ne * nrb, width), I32),
        grid=(b // gb,),
        in_specs=[pl.BlockSpec((gb, ne, n), lambda bi: (bi, 0, 0)), _const_spec((cb, cb))],
        out_specs=pl.BlockSpec((gb, ne * nrb, width), lambda bi: (bi, 0, 0)),
        scratch_shapes=[pltpu.VMEM((gb * ne, n), F32), pltpu.VMEM((ne, SUBLANES, n), F32)],
        compiler_params=_cparams(("arbitrary",)),
        name="topk",
    )(afft, consts["ut"][:cb, :cb])
    return idx.reshape(b, ne, cap)


def _ffn_body(idx_ref, idx_next_ref, hx_ref, wg_ref, wu_ref, wd_ref, y_ref, xs_ref, *, rows):
    e = pl.program_id(1)

    def gather_row(src_ref, slot, j, k):
        t = src_ref[0, 0, j * SUBLANES + k]
        xs_ref[slot, j, pl.ds(k, 1), :] = hx_ref[0, pl.ds(t, 1), :]

    @pl.when(e == 0)
    def _gather_first_expert():
        def body(j, carry):
            for k in range(SUBLANES):
                gather_row(idx_ref, 0, j, k)
            return carry
        lax.fori_loop(0, rows // SUBLANES, body, 0)

    slot = lax.rem(e, 2)
    half = D_MODEL // 2
    gathered = xs_ref[slot].reshape(rows, GATHER_W)
    for j in range(rows // SUBLANES):
        for k in range(SUBLANES):
            gather_row(idx_next_ref, 1 - slot, j, k)
    words = gathered[:, 0:half]
    aff = lax.bitcast_convert_type(gathered[:, half:half + LANES], F32)
    x_lo = lax.bitcast_convert_type(words << 16, F32).astype(BF16)
    x_hi = lax.bitcast_convert_type(words & jnp.uint32(0xFFFF0000), F32).astype(BF16)

    def up_proj(w_ref):
        return (_mm(x_lo, w_ref[0, 0:half, :])
                + _mm(x_hi, w_ref[0, half:, :]))

    gate = up_proj(wg_ref)
    hidden = ((gate * jax.nn.sigmoid(gate)) * up_proj(wu_ref)).astype(BF16)
    y = _mm(hidden, wd_ref[0])
    lane = lax.broadcasted_iota(I32, (rows, LANES), 1)
    val = jnp.sum(jnp.where(lane == e, aff, 0.0), axis=1, keepdims=True)
    y_ref[0, 0] = (y * val).reshape(rows // SUBLANES, SUBLANES, D_MODEL)


def _moe_ffn(idx_steps, hx, lw, group):
    ng, ne, rows = idx_steps.shape
    gtok = hx.shape[1] * group
    hx = hx.reshape(ng, gtok, GATHER_W)
    idx_flat = idx_steps.reshape(ng * ne, 1, rows)
    w_spec = lambda r, c: pl.BlockSpec((1, r, c), lambda g, e: (e, 0, 0))
    idx_spec = lambda step: pl.BlockSpec((1, 1, rows), lambda g, e: (g * ne + jnp.minimum(e + step, ne - 1), 0, 0),
                                         memory_space=pltpu.SMEM)
    return pl.pallas_call(
        functools.partial(_ffn_body, rows=rows),
        out_shape=jax.ShapeDtypeStruct((ng, ne, rows // SUBLANES, SUBLANES, D_MODEL), F32),
        grid=(ng, ne),
        in_specs=[idx_spec(0), idx_spec(1), pl.BlockSpec((1, gtok, GATHER_W), lambda g, e: (g, 0, 0)),
                  w_spec(D_MODEL, D_EXPERT), w_spec(D_MODEL, D_EXPERT), w_spec(D_EXPERT, D_MODEL)],
        out_specs=pl.BlockSpec((1, 1, rows // SUBLANES, SUBLANES, D_MODEL), lambda g, e: (g, e, 0, 0, 0)),
        scratch_shapes=[pltpu.VMEM((2, rows // SUBLANES, SUBLANES, GATHER_W), U32)],
        compiler_params=_cparams(("arbitrary", "arbitrary")),
        name="moe_ffn",
    )(idx_flat, idx_flat, hx, lw["w_gate"], lw["w_up"], lw["w_down"])


def _combine_body(idx_ref, y_ref, o_ref, *, rows):
    @pl.when(pl.program_id(1) == 0)
    def _zero():
        o_ref[...] = jnp.zeros(o_ref.shape, F32)

    def scatter(j, carry):
        toks = [idx_ref[0, 0, j * SUBLANES + k] for k in range(SUBLANES)]
        vals = [o_ref[0, pl.ds(toks[k], 1), :] + y_ref[0, 0, j, pl.ds(k, 1), :] for k in range(SUBLANES)]
        for k in range(SUBLANES):
            o_ref[0, pl.ds(toks[k], 1), :] = vals[k]
        return carry

    lax.fori_loop(0, rows // SUBLANES, scatter, 0)


def _combine(idx_steps, ysel, gtok):
    ng, ne, rows = idx_steps.shape
    return pl.pallas_call(
        functools.partial(_combine_body, rows=rows),
        out_shape=jax.ShapeDtypeStruct((ng, gtok, D_MODEL), F32),
        grid=(ng, ne),
        in_specs=[pl.BlockSpec((1, 1, rows), lambda g, e: (g * ne + e, 0, 0), memory_space=pltpu.SMEM),
                  pl.BlockSpec((1, 1, rows // SUBLANES, SUBLANES, D_MODEL), lambda g, e: (g, e, 0, 0, 0))],
        out_specs=pl.BlockSpec((1, gtok, D_MODEL), lambda g, e: (g, 0, 0)),
        compiler_params=_cparams(("arbitrary", "arbitrary")),
        name="moe_combine",
    )(idx_steps.reshape(ng * ne, 1, rows), ysel)


def _expert_choice(hx, afft, lw, consts, group):
    b, n, _ = hx.shape
    idx = _topk(afft, consts)
    cap = idx.shape[-1]
    ng = b // group
    offs = (jnp.arange(b, dtype=I32) % group * n).reshape(ng, group, 1, 1)
    idx_steps = (idx.reshape(ng, group, N_EXPERTS, cap) + offs).transpose(0, 2, 1, 3).reshape(ng, N_EXPERTS, group * cap)
    ysel = _moe_ffn(idx_steps, hx, lw, group)
    moe = _combine(idx_steps, ysel, group * n)
    return moe.reshape(b, n, D_MODEL)


def _final_body(x1_ref, moe_ref, mod_ref, g_ref, o_ref):
    d = D_MODEL
    x = x1_ref[0] + mod_ref[0, :, 5 * d:6 * d] * moe_ref[0]
    o_ref[0] = _rms(x) * g_ref[...]


def _final_norm(x1, moe, mod_l, row_of_batch, g):
    b, n, d = x1.shape
    tm = min(FINAL_TILE, n)
    tok = pl.BlockSpec((1, tm, d), lambda bi, i: (bi, i, 0))
    return pl.pallas_call(
        _final_body,
        out_shape=jax.ShapeDtypeStruct((b, n, d), F32),
        grid=(b, n // tm),
        in_specs=[tok, tok, pl.BlockSpec((1, 1, 6 * d), lambda bi, i: (row_of_batch(bi), 0, 0)), _const_spec((1, d))],
        out_specs=tok,
        compiler_params=_cparams(("arbitrary", "arbitrary")),
        name="final_norm",
    )(x1, moe, mod_l, g)


def _rope_tables(n_lat):
    half = AXIS_DIM // 2
    inv_freq = ROPE_THETA ** (-jnp.arange(half, dtype=F32) / half)
    t = jnp.arange(n_lat, dtype=jnp.int32)
    row = (t // GRID_W).astype(F32)
    col = (t % GRID_W).astype(F32)
    lane = np.arange(LANES)
    dim = lane % HEAD_DIM
    use_col = jnp.asarray(dim >= AXIS_DIM)
    freq = inv_freq[jnp.asarray(dim % half)]
    pos = jnp.where(use_col[None, :], col[:, None], row[:, None])
    ang = pos * freq[None, :]
    sign = jnp.asarray(np.where(dim % AXIS_DIM < half, -1.0, 1.0), dtype=F32)
    return jnp.cos(ang), jnp.sin(ang) * sign[None, :]


def _block_diag_ones(width):
    seg = np.arange(width) // HEAD_DIM
    return jnp.asarray(seg[:, None] == seg[None, :], dtype=BF16)


def kernel(x_prompt, x_sample, cache_k, cache_v, c, c_ctx, w_mod, b_mod, norm1, norm2, w_in, q_norm, k_norm,
           conv_w, w_conv_out, w_attn_out, w_o, w_router, w_gate, w_up, w_down, final_norm):
    depth = w_mod.shape[0]
    bc, nc, d = x_prompt.shape
    bl, nl, _ = x_sample.shape
    p_len = cache_k.shape[2]
    assert d == D_MODEL and bl + 1 <= MOD_ROWS
    assert nc % min(ROW_CHAIN, nc) == 0 and nl % KEY_BLOCK == 0 and p_len % LANES == 0

    cmat = jnp.zeros((MOD_ROWS, d), F32).at[0].set(c_ctx).at[1:1 + bl].set(c)
    mods = _modulation(cmat, w_mod, b_mod).reshape(depth, MOD_ROWS, 1, 6 * d)

    cos_t, sin_t = _rope_tables(nl)
    ut = np.arange(CUMSUM_BLOCK)
    consts = {"bdq": _block_diag_ones(Q_W), "bdk": _block_diag_ones(KV_W), "cos": cos_t, "sin": sin_t,
              "ut": jnp.asarray(ut[:, None] <= ut[None, :], dtype=BF16)}

    kc_rep = jnp.tile(cache_k.transpose(0, 1, 3, 2, 4), (1, 1, 1, 1, GROUP)).astype(BF16)
    vc_t = cache_v.transpose(0, 1, 3, 4, 2).reshape(bl, depth, KV_W, p_len).astype(BF16)

    ctx_row = lambda bi: 0
    lat_row = lambda bi: bi + 1
    ctx_group = min(CTX_GROUP, bc)

    expert_w = [w.reshape(depth, N_EXPERTS * w.shape[2], w.shape[3]) for w in (w_gate, w_up, w_down)]
    state = {"ctx": (x_prompt, None), "lat": (x_sample, None)}
    new_k, new_v = [], []
    for l in range(depth):
        lw = {
            "norm1": norm1[l].reshape(1, d), "norm2": norm2[l].reshape(1, d),
            "w_in": w_in[l].astype(BF16),
            "gq": jnp.tile(q_norm[l], N_HEADS).reshape(1, Q_W), "gk": jnp.tile(k_norm[l], N_KV_HEADS).reshape(1, KV_W),
            "conv_w": jnp.zeros((SUBLANES, CONV_W), F32).at[0:3].set(conv_w[l]),
            "w_conv_out": w_conv_out[l].astype(BF16), "w_attn_out": w_attn_out[l].astype(BF16),
            "w_o": w_o[l].astype(BF16),
            "w_router": jnp.zeros((d, LANES), F32).at[:, 0:N_EXPERTS].set(w_router[l]),
        }
        mod_l = mods[l]
        modp = mods[l - 1] if l else None
        xa, xb = state["lat"]
        lat_res = _premix(xa, xb, modp if xb is not None else None, mod_l, lat_row, lw, consts, True,
                          cast=[(w, l) for w in expert_w])
        for key, w16, w32 in zip(("w_gate", "w_up", "w_down"), lat_res[:3], (w_gate, w_up, w_down)):
            lw[key] = w16.reshape(w32.shape[1:])
        for name in ("ctx", "lat"):
            is_lat = name == "lat"
            row_fn = lat_row if is_lat else ctx_row
            xa, xb = state[name]
            if is_lat:
                res = lat_res[3:]
            else:
                res = _premix(xa, xb, modp if xb is not None else None, mod_l, row_fn, lw, consts, is_lat)
            q, krep, vt = res[0:3]
            pos = 3
            if not is_lat:
                new_k.append(res[3])
                new_v.append(res[4])
                pos = 5
            u, gb, sga, sgat = res[pos:pos + 4]
            x_cur = res[pos + 4] if xb is not None else xa
            if is_lat:
                attn = _attention(q, krep, vt, kc_rep[:, l], vc_t[:, l])
            else:
                attn = _attention(q, krep, vt)
            x1, hx, afft = _postmix(attn, u, gb, sga, sgat, x_cur, mod_l, row_fn, lw)
            moe = _expert_choice(hx, afft, lw, consts, group=1 if is_lat else ctx_group)
            state[name] = (x1, moe)

    g = final_norm.reshape(1, d)
    y_prompt = _final_norm(*state["ctx"], mods[depth - 1], ctx_row, g)
    y_sample = _final_norm(*state["lat"], mods[depth - 1], lat_row, g)
    shape_kv = (bc, depth, nc, N_KV_HEADS, HEAD_DIM)
    new_cache_k = jnp.stack(new_k, axis=1).reshape(shape_kv)
    new_cache_v = jnp.stack(new_v, axis=1).reshape(shape_kv)
    return (y_prompt, y_sample, new_cache_k, new_cache_v)
```

```python
import functools

import jax
import jax.numpy as jnp
import numpy as np
from jax import lax
from jax.experimental import pallas as pl
from jax.experimental.pallas import tpu as pltpu

F32 = jnp.float32
BF16 = jnp.bfloat16
I32 = jnp.int32
U32 = jnp.uint32

D_MODEL = 1024
N_HEADS = 16
N_KV_HEADS = 4
HEAD_DIM = 64
GROUP = N_HEADS // N_KV_HEADS
Q_W = N_HEADS * HEAD_DIM
KV_W = N_KV_HEADS * HEAD_DIM
CONV_W = D_MODEL // 2
IN_W = Q_W + 2 * KV_W + 3 * CONV_W + 2 * D_MODEL
N_EXPERTS = 16
EC_FACTOR = 2
D_EXPERT = 1024
GRID_W = 64
AXIS_DIM = HEAD_DIM // 2
ROPE_THETA = 10000.0
EPS = 1e-6
MOD_ROWS = 16

LANES = 128
SUBLANES = 8
VMEM_LIMIT_BYTES = 56 * 1024 * 1024

MIX_TILE = 512
ROW_CHAIN = 256
FINAL_TILE = 1024
KEY_BLOCK = 1024
QUERY_TILE = 512
CUMSUM_BLOCK = 256
RANK_BLOCK = 128
TOPK_ROWS = 512 * 1024
CTX_GROUP = 16
ONES_ROWS = 16
GATHER_W = D_MODEL // 2 + LANES
LOG2_E = 1.4426950408889634


def _cparams(sem):
    return pltpu.CompilerParams(dimension_semantics=sem, vmem_limit_bytes=VMEM_LIMIT_BYTES)


def _mm(a, b):
    return jnp.dot(a, b, preferred_element_type=F32)


def _mm_tn(a_t, b):
    return lax.dot_general(a_t, b, (((0,), (0,)), ((), ())), preferred_element_type=F32)


def _const_spec(shape, single=False):
    nd = len(shape)
    return pl.BlockSpec(shape, lambda *_: (0,) * nd, pipeline_mode=pl.Buffered(1) if single else None)


def _mod_body(c_ref, w_ref, b_ref, o_ref):
    c = c_ref[...]
    s = c * jax.nn.sigmoid(c)
    o_ref[0] = _mm(s.astype(BF16), w_ref[0].astype(BF16)) + b_ref[0]


def _modulation(cmat, w_mod, b_mod):
    depth = w_mod.shape[0]
    tn = 1536
    return pl.pallas_call(
        _mod_body,
        out_shape=jax.ShapeDtypeStruct((depth, MOD_ROWS, 6 * D_MODEL), F32),
        grid=(depth, 6 * D_MODEL // tn),
        in_specs=[
            pl.BlockSpec((MOD_ROWS, D_MODEL), lambda l, j: (0, 0)),
            pl.BlockSpec((1, D_MODEL, tn), lambda l, j: (l, 0, j)),
            pl.BlockSpec((1, 1, tn), lambda l, j: (l, 0, j)),
        ],
        out_specs=pl.BlockSpec((1, MOD_ROWS, tn), lambda l, j: (l, 0, j)),
        compiler_params=_cparams(("arbitrary", "arbitrary")),
        name="modulation",
    )(cmat, w_mod, b_mod.reshape(depth, 1, 6 * D_MODEL))


def _rms(x):
    return x * lax.rsqrt(jnp.mean(x * x, axis=-1, keepdims=True) + EPS)


def _premix_body(*refs, is_lat, has_prev, tm, n_cast):
    it = iter(refs)
    xa_ref = next(it)
    if has_prev:
        xb_ref = next(it)
        modp_ref = next(it)
    mod_ref = next(it)
    n1_ref = next(it)
    win_ref = next(it)
    gq_ref = next(it)
    gk_ref = next(it)
    bdq_ref = next(it)
    bdk_ref = next(it)
    if is_lat:
        cos_ref = next(it)
        sin_ref = next(it)
    cast_in = [next(it) for _ in range(n_cast)]
    cast_out = [next(it) for _ in range(n_cast)]
    for src, dst in zip(cast_in, cast_out):
        dst[...] = src[0].astype(BF16)
    q_ref = next(it)
    krep_ref = next(it)
    vt_ref = next(it)
    if not is_lat:
        kc_ref = next(it)
        vc_ref = next(it)
    u_ref = next(it)
    gb_ref = next(it)
    sga_ref = next(it)
    sgat_ref = next(it)
    if has_prev:
        xn_ref = next(it)

    d = D_MODEL
    sub = min(ROW_CHAIN, tm)
    lane = lax.broadcasted_iota(I32, (sub, LANES), 1)
    first_half = (lane % AXIS_DIM) < (AXIS_DIM // 2)
    low_head = lane < HEAD_DIM
    sh1 = mod_ref[0, :, 0:d]
    sc1 = mod_ref[0, :, d:2 * d]

    def row_chain(rows):
        x = xa_ref[0, rows, :]
        if has_prev:
            x = x + modp_ref[0, :, 5 * d:6 * d] * xb_ref[0, rows, :]
            xn_ref[0, rows, :] = x
        h = (_rms(x) * n1_ref[...]) * (1.0 + sc1) + sh1
        hb = h.astype(BF16)

        def proj(lo, hi):
            return _mm(hb, win_ref[:, lo:hi])

        def rope(chunk):
            if not is_lat:
                return chunk
            partner = jnp.where(first_half, pltpu.roll(chunk, LANES - AXIS_DIM // 2, 1),
                                pltpu.roll(chunk, AXIS_DIM // 2, 1))
            return chunk * cos_ref[rows, :] + partner * sin_ref[rows, :]

        pq = proj(0, Q_W)
        ssq = _mm((pq * pq).astype(BF16), bdq_ref[...])
        qn = pq * lax.rsqrt(ssq * (1.0 / HEAD_DIM) + EPS) * gq_ref[...]
        for c in range(Q_W // LANES):
            qc = rope(qn[:, c * LANES:(c + 1) * LANES])
            q_ref[0, c * LANES:(c + 1) * LANES, rows] = (qc * (HEAD_DIM ** -0.5 * LOG2_E)).T.astype(BF16)

        pk = proj(Q_W, Q_W + KV_W)
        ssk = _mm((pk * pk).astype(BF16), bdk_ref[...])
        kn = pk * lax.rsqrt(ssk * (1.0 / HEAD_DIM) + EPS) * gk_ref[...]
        if not is_lat:
            kc_ref[0, rows, :] = kn
        for c in range(KV_W // LANES):
            kc = rope(kn[:, c * LANES:(c + 1) * LANES])
            rolled = pltpu.roll(kc, HEAD_DIM, 1)
            even = jnp.where(low_head, kc, rolled).astype(BF16)
            odd = jnp.where(low_head, rolled, kc).astype(BF16)
            for s in range(KV_W // LANES):
                krep_ref[0, 2 * c, rows, s * LANES:(s + 1) * LANES] = even
                krep_ref[0, 2 * c + 1, rows, s * LANES:(s + 1) * LANES] = odd

        pv = proj(Q_W + KV_W, Q_W + 2 * KV_W)
        if not is_lat:
            vc_ref[0, rows, :] = pv
        vt_ref[0, 0, :, rows] = pv.T.astype(BF16)

        o = Q_W + 2 * KV_W
        gb_ref[0, rows, :] = proj(o, o + CONV_W).astype(BF16)
        u_ref[0, rows, :] = proj(o + CONV_W, o + 2 * CONV_W) * proj(o + 2 * CONV_W, o + 3 * CONV_W)
        o = o + 3 * CONV_W
        sga_ref[0, rows, :] = jax.nn.sigmoid(proj(o, o + d)).astype(BF16)
        sgat_ref[0, rows, :] = jax.nn.sigmoid(proj(o + d, o + 2 * d)).astype(BF16)

    for c in range(tm // sub):
        row_chain(slice(c * sub, (c + 1) * sub))


def _premix(xa, xb, modp, mod_l, row_of_batch, lw, consts, is_lat, cast=()):
    b, n, d = xa.shape
    tm = min(MIX_TILE, n)
    tk = min(KEY_BLOCK, n)
    has_prev = xb is not None
    sub = tk // tm
    steps = n // tm
    tok_spec = lambda w: pl.BlockSpec((1, tm, w), lambda bi, i: (bi, i, 0))
    mod_spec = pl.BlockSpec((1, 1, 6 * d), lambda bi, i: (row_of_batch(bi), 0, 0))

    ins, specs = [xa], [tok_spec(d)]
    if has_prev:
        ins += [xb, modp]
        specs += [tok_spec(d), mod_spec]
    ins += [mod_l, lw["norm1"], lw["w_in"], lw["gq"], lw["gk"], consts["bdq"], consts["bdk"]]
    specs += [mod_spec, _const_spec((1, d)), _const_spec((d, IN_W), single=True), _const_spec((1, Q_W)),
              _const_spec((1, KV_W)), _const_spec((Q_W, Q_W), single=True), _const_spec((KV_W, KV_W))]
    if is_lat:
        ins += [consts["cos"], consts["sin"]]
        specs += [pl.BlockSpec((tm, LANES), lambda bi, i: (i, 0))] * 2

    outs, ospecs = [], []
    for w, layer in cast:
        _, wrows, wcols = w.shape
        slab = wrows // (b * steps)
        assert slab * b * steps == wrows and slab % SUBLANES == 0
        ins.append(w)
        specs.append(pl.BlockSpec((1, slab, wcols), lambda bi, i, layer=layer: (layer, bi * steps + i, 0)))
        outs.append(jax.ShapeDtypeStruct((wrows, wcols), BF16))
        ospecs.append(pl.BlockSpec((slab, wcols), lambda bi, i: (bi * steps + i, 0)))
    outs += [jax.ShapeDtypeStruct((b, Q_W, n), BF16),
             jax.ShapeDtypeStruct((b, N_KV_HEADS, n, KV_W), BF16),
             jax.ShapeDtypeStruct((b, n // tk, KV_W, tk), BF16)]
    ospecs += [pl.BlockSpec((1, Q_W, tm), lambda bi, i: (bi, 0, i)),
               pl.BlockSpec((1, N_KV_HEADS, tm, KV_W), lambda bi, i: (bi, 0, i, 0)),
               pl.BlockSpec((1, 1, KV_W, tm), lambda bi, i: (bi, i // sub, 0, i % sub))]
    if not is_lat:
        outs += [jax.ShapeDtypeStruct((b, n, KV_W), F32)] * 2
        ospecs += [tok_spec(KV_W)] * 2
    outs += [jax.ShapeDtypeStruct((b, n, CONV_W), F32), jax.ShapeDtypeStruct((b, n, CONV_W), BF16),
             jax.ShapeDtypeStruct((b, n, d), BF16), jax.ShapeDtypeStruct((b, n, d), BF16)]
    ospecs += [tok_spec(CONV_W), tok_spec(CONV_W), tok_spec(d), tok_spec(d)]
    if has_prev:
        outs.append(jax.ShapeDtypeStruct((b, n, d), F32))
        ospecs.append(tok_spec(d))

    res = pl.pallas_call(
        functools.partial(_premix_body, is_lat=is_lat, has_prev=has_prev, tm=tm, n_cast=len(cast)),
        out_shape=outs, grid=(b, n // tm), in_specs=specs, out_specs=ospecs,
        compiler_params=_cparams(("arbitrary", "arbitrary")),
        name="premix_lat" if is_lat else "premix_ctx",
    )(*ins)
    return list(res)


def _attn_body(*refs, p_len, n, tq, tk, hps):
    it = iter(refs)
    q_ref = next(it)
    if p_len:
        kc_ref = next(it)
        vc_ref = next(it)
    k_ref = next(it)
    v_ref = next(it)
    o_ref = next(it)
    qbd_ref = next(it)
    s_ref = next(it)
    m_ref = next(it)
    acc_ref = next(it)

    m_ref[...] = jnp.full(m_ref.shape, -jnp.inf, F32)
    acc_ref[...] = jnp.zeros(acc_ref.shape, F32)
    nb = n // tk

    def one_kv_head(hh):
        base = hh * GROUP
        vrows = slice(hh * HEAD_DIM, (hh + 1) * HEAD_DIM)
        lanes = slice(hh * KV_W, (hh + 1) * KV_W)

        for g in range(GROUP):
            band = q_ref[0, hh * KV_W + g * HEAD_DIM:hh * KV_W + (g + 1) * HEAD_DIM, :]
            pieces = [jnp.zeros((g * HEAD_DIM, tq), BF16), band, jnp.zeros(((GROUP - 1 - g) * HEAD_DIM, tq), BF16)]
            qbd_ref[base + g] = jnp.concatenate([piece for piece in pieces if piece.shape[0]], axis=0)

        def score(parts, g):
            off, m_blk = 0, None
            for get_k, _, size in parts:
                s = _mm(get_k(), qbd_ref[base + g])
                s_ref[base + g, off:off + size, :] = s
                m_part = jnp.max(s, axis=0, keepdims=True)
                m_blk = m_part if m_blk is None else jnp.maximum(m_blk, m_part)
                off += size
            return m_blk

        def softmax_pv(parts, g, m_blk):
            size = sum(part[2] for part in parts)
            m_prev = m_ref[base + g]
            m_new = jnp.maximum(m_prev, m_blk)
            alpha = jnp.exp2(m_prev - m_new)
            p = jnp.exp2(s_ref[base + g, 0:size, :] - m_new).astype(BF16)
            v_ext = jnp.concatenate([jnp.concatenate([get_v() for _, get_v, _ in parts], axis=1),
                                     jnp.ones((ONES_ROWS, size), BF16)], axis=0)
            acc_ref[base + g] = alpha * acc_ref[base + g] + _mm(v_ext, p)
            m_ref[base + g] = m_new

        def key_block(parts, m_blk, next_parts):
            for g in range(GROUP):
                if g + 1 < GROUP:
                    m_next = score(parts, g + 1)
                elif next_parts is not None:
                    m_next = score(next_parts, 0)
                else:
                    m_next = None
                softmax_pv(parts, g, m_blk)
                m_blk = m_next
            return m_blk

        def block_parts(i):
            if isinstance(i, int):
                parts = [(lambda: k_ref[0, hh, i * tk:(i + 1) * tk, :], lambda: v_ref[0, i, vrows, :], tk)]
                if p_len and i == 0:
                    parts.append((lambda: kc_ref[0, hh], lambda: vc_ref[0, vrows, :], p_len))
                return parts
            return [(lambda: k_ref[0, hh, pl.ds(pl.multiple_of(i * tk, tk), tk), :], lambda: v_ref[0, i, vrows, :], tk)]

        m_blk = score(block_parts(0), 0)
        if nb > 1:
            m_blk = key_block(block_parts(0), m_blk, block_parts(1))
            m_blk = lax.fori_loop(1, nb - 1, lambda i, m: key_block(block_parts(i), m, block_parts(i + 1)), m_blk)
        key_block(block_parts(nb - 1), m_blk, None)

        outs = []
        for g in range(GROUP):
            acc = acc_ref[base + g]
            outs.append(acc[0:HEAD_DIM, :] * (1.0 / acc[HEAD_DIM:HEAD_DIM + 1, :]))
        o_ref[0, lanes, :] = jnp.concatenate(outs, axis=0).astype(BF16)

    for hh in range(hps):
        one_kv_head(hh)


def _attention(q, krep, vt, kc_rep=None, vc_t=None):
    b, _, n = q.shape
    tk = vt.shape[-1]
    tq = min(QUERY_TILE, n)
    p_len = 0 if kc_rep is None else kc_rep.shape[2]
    hps = N_KV_HEADS if n == tk else 1
    ins, specs = [q], [pl.BlockSpec((1, hps * KV_W, tq), lambda bi, h, i: (bi, h, i))]
    if p_len:
        ins += [kc_rep, vc_t]
        specs += [pl.BlockSpec((1, hps, p_len, KV_W), lambda bi, h, i: (bi, h, 0, 0)),
                  pl.BlockSpec((1, hps * HEAD_DIM, p_len), lambda bi, h, i: (bi, h, 0))]
    ins += [krep, vt]
    specs += [pl.BlockSpec((1, hps, n, KV_W), lambda bi, h, i: (bi, h, 0, 0)),
              pl.BlockSpec((1, n // tk, hps * HEAD_DIM, tk), lambda bi, h, i: (bi, 0, h, 0))]
    return pl.pallas_call(
        functools.partial(_attn_body, p_len=p_len, n=n, tq=tq, tk=tk, hps=hps),
        out_shape=jax.ShapeDtypeStruct((b, Q_W, n), BF16),
        grid=(b, N_KV_HEADS // hps, n // tq),
        in_specs=specs,
        out_specs=pl.BlockSpec((1, hps * KV_W, tq), lambda bi, h, i: (bi, h, i)),
        scratch_shapes=[pltpu.VMEM((hps * GROUP, KV_W, tq), BF16),
                        pltpu.VMEM((hps * GROUP, tk + p_len, tq), F32),
                        pltpu.VMEM((hps * GROUP, 1, tq), F32),
                        pltpu.VMEM((hps * GROUP, HEAD_DIM + ONES_ROWS, tq), F32)],
        compiler_params=_cparams(("arbitrary", "arbitrary", "arbitrary")),
        name="attention_lat" if p_len else "attention_ctx",
    )(*ins)


def _split_bf16(x):
    hi = x.astype(BF16)
    lo = (x - hi.astype(F32)).astype(BF16)
    return hi, lo


def _postmix_body(attn_ref, u_ref, up_ref, un_ref, gb_ref, sga_ref, sgat_ref, x_ref, mod_ref, n2_ref, cw_ref,
                  wa_ref, wc_ref, wo_ref, wr_ref, x1_ref, hx_ref, afft_ref, *, tm):
    d = D_MODEL
    i = pl.program_id(1)
    last = pl.num_programs(1) - 1
    u = u_ref[0]
    row = lax.broadcasted_iota(I32, (tm, CONV_W), 0)
    prev_row = jnp.where(i > 0, up_ref[0, SUBLANES - 1:SUBLANES, :], 0.0)
    next_row = jnp.where(i < last, un_ref[0, 0:1, :], 0.0)
    u_m1 = jnp.where(row == 0, prev_row, pltpu.roll(u, 1, 0))
    u_p1 = jnp.where(row == tm - 1, next_row, pltpu.roll(u, tm - 1, 0))
    conv = cw_ref[0:1, :] * u_m1 + cw_ref[1:2, :] * u + cw_ref[2:3, :] * u_p1
    cv = (gb_ref[0].astype(F32) * conv).astype(BF16)
    r_hi, r_lo = _split_bf16(wr_ref[...])
    half = d // 2
    sub = min(ROW_CHAIN, tm)
    lane = lax.broadcasted_iota(I32, (sub, LANES), 1)
    valid = lane < N_EXPERTS

    for c in range(tm // sub):
        rows = slice(c * sub, (c + 1) * sub)
        conv_o = _mm(cv[rows], wc_ref[...])
        attn_o = _mm_tn(attn_ref[0, :, rows], wa_ref[...])
        merged = sga_ref[0, rows, :].astype(F32) * conv_o + sgat_ref[0, rows, :].astype(F32) * attn_o
        mix = _mm(merged.astype(BF16), wo_ref[...])
        x1 = x_ref[0, rows, :] + mod_ref[0, :, 2 * d:3 * d] * mix
        x1_ref[0, rows, :] = x1
        h2 = (_rms(x1) * n2_ref[...]) * (1.0 + mod_ref[0, :, 4 * d:5 * d]) + mod_ref[0, :, 3 * d:4 * d]

        h_hi, h_lo = _split_bf16(h2)
        w_lo = lax.bitcast_convert_type(h_hi[:, :half].astype(F32), U32) >> 16
        w_hi = lax.bitcast_convert_type(h_hi[:, half:].astype(F32), U32) & jnp.uint32(0xFFFF0000)
        hx_ref[0, rows, 0:half] = w_lo | w_hi

        logits = (_mm(h_hi, r_hi) + _mm(h_hi, r_lo)
                  + _mm(h_lo, r_hi))
        logits = jnp.where(valid, logits, -jnp.inf)
        e = jnp.exp(logits - jnp.max(logits, axis=-1, keepdims=True))
        e = jnp.where(valid, e, 0.0)
        aff = e / jnp.sum(e, axis=-1, keepdims=True)
        hx_ref[0, rows, half:half + LANES] = lax.bitcast_convert_type(aff, U32)
        afft_ref[0, :, rows] = aff.T[0:N_EXPERTS, :]


def _postmix(attn, u, gb, sga, sgat, x, mod_l, row_of_batch, lw):
    b, n, d = x.shape
    tm = min(MIX_TILE, n)
    nsub = n // SUBLANES
    per = tm // SUBLANES
    tok_spec = lambda w: pl.BlockSpec((1, tm, w), lambda bi, i: (bi, i, 0))
    specs = [pl.BlockSpec((1, Q_W, tm), lambda bi, i: (bi, 0, i)), tok_spec(CONV_W),
             pl.BlockSpec((1, SUBLANES, CONV_W), lambda bi, i: (bi, jnp.maximum(i * per - 1, 0), 0)),
             pl.BlockSpec((1, SUBLANES, CONV_W), lambda bi, i: (bi, jnp.minimum((i + 1) * per, nsub - 1), 0)),
             tok_spec(CONV_W), tok_spec(d), tok_spec(d), tok_spec(d),
             pl.BlockSpec((1, 1, 6 * d), lambda bi, i: (row_of_batch(bi), 0, 0)),
             _const_spec((1, d)), _const_spec((SUBLANES, CONV_W)),
             _const_spec((Q_W, d)), _const_spec((CONV_W, d)), _const_spec((d, d)), _const_spec((d, LANES))]
    outs = [jax.ShapeDtypeStruct((b, n, d), F32), jax.ShapeDtypeStruct((b, n, GATHER_W), U32),
            jax.ShapeDtypeStruct((b, N_EXPERTS, n), F32)]
    ospecs = [tok_spec(d), tok_spec(GATHER_W),
              pl.BlockSpec((1, N_EXPERTS, tm), lambda bi, i: (bi, 0, i))]
    return pl.pallas_call(
        functools.partial(_postmix_body, tm=tm),
        out_shape=outs, grid=(b, n // tm), in_specs=specs, out_specs=ospecs,
        compiler_params=_cparams(("arbitrary", "arbitrary")),
        name="postmix",
    )(attn, u, u, u, gb, sga, sgat, x, mod_l, lw["norm2"], lw["conv_w"], lw["w_attn_out"], lw["w_conv_out"],
      lw["w_o"], lw["w_router"])


def _topk_body(aff_ref, ut_ref, idx_ref, sel_ref, pos_ref, *, gb, n, cap):
    ne = N_EXPERTS
    rows_all = gb * ne
    a = aff_ref[...].reshape(rows_all, n)
    bits = lax.bitcast_convert_type(a, I32)

    def count(mask):
        return jnp.sum(jnp.where(mask, 1.0, 0.0), axis=1, keepdims=True)

    def thr_step(t, thr):
        cand = thr | (jnp.int32(1) << (30 - t))
        return jnp.where(count(bits >= cand) >= cap, cand, thr)

    thr = lax.fori_loop(0, 31, thr_step, jnp.zeros((rows_all, 1), I32))
    gt = bits > thr
    eq = bits == thr
    need = cap - count(gt)
    tok = lax.broadcasted_iota(I32, (rows_all, n), 1)
    nbits = int(np.log2(n)) + 1

    def tie_step(t, bound):
        cand = bound + (jnp.int32(1) << (nbits - 1 - t))
        ok = (cand <= n) & (count(eq & (tok < cand)) <= need)
        return jnp.where(ok, cand, bound)

    bound = lax.fori_loop(0, nbits, tie_step, jnp.zeros((rows_all, 1), I32))
    sel_ref[...] = jnp.where(gt | (eq & (tok < bound)), 1.0, 0.0)

    cb = min(CUMSUM_BLOCK, n)
    nblk = n // cb
    nrb = max(cap // RANK_BLOCK, 1)
    width = min(cap, RANK_BLOCK)
    tiles = RANK_BLOCK // SUBLANES
    lane = lax.broadcasted_iota(I32, (RANK_BLOCK, LANES), 1)
    rank0 = (lax.broadcasted_iota(I32, (tiles, SUBLANES, LANES), 0) * SUBLANES
             + lax.broadcasted_iota(I32, (tiles, SUBLANES, LANES), 1)).astype(F32)

    def batch_step(bi, carry):
        sel = sel_ref[pl.ds(pl.multiple_of(bi * ne, ne), ne), :]
        off = jnp.zeros((ne, 1), F32)
        for k in range(nblk):
            blk = _mm(sel[:, k * cb:(k + 1) * cb].astype(BF16), ut_ref[...])
            pos = blk + off
            for ex in range(ne):
                pos_ref[ex, :, k * cb:(k + 1) * cb] = jnp.broadcast_to(pos[ex:ex + 1, :], (SUBLANES, cb))
            off = off + blk[:, cb - 1:cb]

        def expert_step(ei, cols):
            for rb in range(nrb):
                rank = rank0 + float(rb * RANK_BLOCK)
                cnt = jnp.zeros((tiles, SUBLANES, LANES), F32)
                for lc in range(n // LANES):
                    prow = pos_ref[ei, :, lc * LANES:(lc + 1) * LANES]
                    cnt = cnt + jnp.where(prow[None] <= rank, 1.0, 0.0)
                col = jnp.sum(cnt.reshape(RANK_BLOCK, LANES), axis=1, keepdims=True)
                cols = jnp.where(lane == ei * nrb + rb, col, cols)
            return cols

        cols = lax.fori_loop(0, ne, expert_step, jnp.zeros((RANK_BLOCK, LANES), F32))
        rows = cols.T
        idx_ref[bi] = rows[0:ne * nrb, 0:width].astype(I32)
        return carry

    lax.fori_loop(0, gb, batch_step, 0)


def _topk(afft, consts):
    b, ne, n = afft.shape
    cap = max(1, EC_FACTOR * n // N_EXPERTS)
    nrb = max(cap // RANK_BLOCK, 1)
    width = min(cap, RANK_BLOCK)
    cb = min(CUMSUM_BLOCK, n)
    gb = min(b, max(1, TOPK_ROWS // (ne * n)))
    assert b % gb == 0
    idx = pl.pallas_call(
        functools.partial(_topk_body, gb=gb, n=n, cap=cap),
        out_shape=jax.ShapeDtypeStruct((b, ne * nrb, width), I32),
        grid=(b // gb,),
        in_specs=[pl.BlockSpec((gb, ne, n), lambda bi: (bi, 0, 0)), _const_spec((cb, cb))],
        out_specs=pl.BlockSpec((gb, ne * nrb, width), lambda bi: (bi, 0, 0)),
        scratch_shapes=[pltpu.VMEM((gb * ne, n), F32), pltpu.VMEM((ne, SUBLANES, n), F32)],
        compiler_params=_cparams(("arbitrary",)),
        name="topk",
    )(afft, consts["ut"][:cb, :cb])
    return idx.reshape(b, ne, cap)


def _ffn_body(idx_ref, idx_next_ref, hx_ref, wg_ref, wu_ref, wd_ref, y_ref, xs_ref, *, rows):
    e = pl.program_id(1)

    def gather_row(src_ref, slot, j, k):
        t = src_ref[0, 0, j * SUBLANES + k]
        xs_ref[slot, j, pl.ds(k, 1), :] = hx_ref[0, pl.ds(t, 1), :]

    @pl.when(e == 0)
    def _gather_first_expert():
        def body(j, carry):
            for k in range(SUBLANES):
                gather_row(idx_ref, 0, j, k)
            return carry
        lax.fori_loop(0, rows // SUBLANES, body, 0)

    slot = lax.rem(e, 2)
    half = D_MODEL // 2
    gathered = xs_ref[slot].reshape(rows, GATHER_W)
    for j in range(rows // SUBLANES):
        for k in range(SUBLANES):
            gather_row(idx_next_ref, 1 - slot, j, k)
    words = gathered[:, 0:half]
    aff = lax.bitcast_convert_type(gathered[:, half:half + LANES], F32)
    x_lo = lax.bitcast_convert_type(words << 16, F32).astype(BF16)
    x_hi = lax.bitcast_convert_type(words & jnp.uint32(0xFFFF0000), F32).astype(BF16)

    def up_proj(w_ref):
        return (_mm(x_lo, w_ref[0, 0:half, :])
                + _mm(x_hi, w_ref[0, half:, :]))

    gate = up_proj(wg_ref)
    hidden = ((gate * jax.nn.sigmoid(gate)) * up_proj(wu_ref)).astype(BF16)
    y = _mm(hidden, wd_ref[0])
    lane = lax.broadcasted_iota(I32, (rows, LANES), 1)
    val = jnp.sum(jnp.where(lane == e, aff, 0.0), axis=1, keepdims=True)
    y_ref[0, 0] = (y * val).reshape(rows // SUBLANES, SUBLANES, D_MODEL)


def _moe_ffn(idx_steps, hx, lw, group):
    ng, ne, rows = idx_steps.shape
    gtok = hx.shape[1] * group
    hx = hx.reshape(ng, gtok, GATHER_W)
    idx_flat = idx_steps.reshape(ng * ne, 1, rows)
    w_spec = lambda r, c: pl.BlockSpec((1, r, c), lambda g, e: (e, 0, 0))
    idx_spec = lambda step: pl.BlockSpec((1, 1, rows), lambda g, e: (g * ne + jnp.minimum(e + step, ne - 1), 0, 0),
                                         memory_space=pltpu.SMEM)
    return pl.pallas_call(
        functools.partial(_ffn_body, rows=rows),
        out_shape=jax.ShapeDtypeStruct((ng, ne, rows // SUBLANES, SUBLANES, D_MODEL), F32),
        grid=(ng, ne),
        in_specs=[idx_spec(0), idx_spec(1), pl.BlockSpec((1, gtok, GATHER_W), lambda g, e: (g, 0, 0)),
                  w_spec(D_MODEL, D_EXPERT), w_spec(D_MODEL, D_EXPERT), w_spec(D_EXPERT, D_MODEL)],
        out_specs=pl.BlockSpec((1, 1, rows // SUBLANES, SUBLANES, D_MODEL), lambda g, e: (g, e, 0, 0, 0)),
        scratch_shapes=[pltpu.VMEM((2, rows // SUBLANES, SUBLANES, GATHER_W), U32)],
        compiler_params=_cparams(("arbitrary", "arbitrary")),
        name="moe_ffn",
    )(idx_flat, idx_flat, hx, lw["w_gate"], lw["w_up"], lw["w_down"])


def _combine_body(idx_ref, y_ref, o_ref, *, rows):
    @pl.when(pl.program_id(1) == 0)
    def _zero():
        o_ref[...] = jnp.zeros(o_ref.shape, F32)

    def scatter(j, carry):
        toks = [idx_ref[0, 0, j * SUBLANES + k] for k in range(SUBLANES)]
        vals = [o_ref[0, pl.ds(toks[k], 1), :] + y_ref[0, 0, j, pl.ds(k, 1), :] for k in range(SUBLANES)]
        for k in range(SUBLANES):
            o_ref[0, pl.ds(toks[k], 1), :] = vals[k]
        return carry

    lax.fori_loop(0, rows // SUBLANES, scatter, 0)


def _combine(idx_steps, ysel, gtok):
    ng, ne, rows = idx_steps.shape
    return pl.pallas_call(
        functools.partial(_combine_body, rows=rows),
        out_shape=jax.ShapeDtypeStruct((ng, gtok, D_MODEL), F32),
        grid=(ng, ne),
        in_specs=[pl.BlockSpec((1, 1, rows), lambda g, e: (g * ne + e, 0, 0), memory_space=pltpu.SMEM),
                  pl.BlockSpec((1, 1, rows // SUBLANES, SUBLANES, D_MODEL), lambda g, e: (g, e, 0, 0, 0))],
        out_specs=pl.BlockSpec((1, gtok, D_MODEL), lambda g, e: (g, 0, 0)),
        compiler_params=_cparams(("arbitrary", "arbitrary")),
        name="moe_combine",
    )(idx_steps.reshape(ng * ne, 1, rows), ysel)


def _expert_choice(hx, afft, lw, consts, group):
    b, n, _ = hx.shape
    idx = _topk(afft, consts)
    cap = idx.shape[-1]
    ng = b // group
    offs = (jnp.arange(b, dtype=I32) % group * n).reshape(ng, group, 1, 1)
    idx_steps = (idx.reshape(ng, group, N_EXPERTS, cap) + offs).transpose(0, 2, 1, 3).reshape(ng, N_EXPERTS, group * cap)
    ysel = _moe_ffn(idx_steps, hx, lw, group)
    moe = _combine(idx_steps, ysel, group * n)
    return moe.reshape(b, n, D_MODEL)


def _final_body(x1_ref, moe_ref, mod_ref, g_ref, o_ref):
    d = D_MODEL
    x = x1_ref[0] + mod_ref[0, :, 5 * d:6 * d] * moe_ref[0]
    o_ref[0] = _rms(x) * g_ref[...]


def _final_norm(x1, moe, mod_l, row_of_batch, g):
    b, n, d = x1.shape
    tm = min(FINAL_TILE, n)
    tok = pl.BlockSpec((1, tm, d), lambda bi, i: (bi, i, 0))
    return pl.pallas_call(
        _final_body,
        out_shape=jax.ShapeDtypeStruct((b, n, d), F32),
        grid=(b, n // tm),
        in_specs=[tok, tok, pl.BlockSpec((1, 1, 6 * d), lambda bi, i: (row_of_batch(bi), 0, 0)), _const_spec((1, d))],
        out_specs=tok,
        compiler_params=_cparams(("arbitrary", "arbitrary")),
        name="final_norm",
    )(x1, moe, mod_l, g)


def _rope_tables(n_lat):
    half = AXIS_DIM // 2
    inv_freq = ROPE_THETA ** (-jnp.arange(half, dtype=F32) / half)
    t = jnp.arange(n_lat, dtype=jnp.int32)
    row = (t // GRID_W).astype(F32)
    col = (t % GRID_W).astype(F32)
    lane = np.arange(LANES)
    dim = lane % HEAD_DIM
    use_col = jnp.asarray(dim >= AXIS_DIM)
    freq = inv_freq[jnp.asarray(dim % half)]
    pos = jnp.where(use_col[None, :], col[:, None], row[:, None])
    ang = pos * freq[None, :]
    sign = jnp.asarray(np.where(dim % AXIS_DIM < half, -1.0, 1.0), dtype=F32)
    return jnp.cos(ang), jnp.sin(ang) * sign[None, :]


def _block_diag_ones(width):
    seg = np.arange(width) // HEAD_DIM
    return jnp.asarray(seg[:, None] == seg[None, :], dtype=BF16)


def kernel(x_prompt, x_sample, cache_k, cache_v, c, c_ctx, w_mod, b_mod, norm1, norm2, w_in, q_norm, k_norm,
           conv_w, w_conv_out, w_attn_out, w_o, w_router, w_gate, w_up, w_down, final_norm):
    depth = w_mod.shape[0]
    bc, nc, d = x_prompt.shape
    bl, nl, _ = x_sample.shape
    p_len = cache_k.shape[2]
    assert d == D_MODEL and bl + 1 <= MOD_ROWS
    assert nc % min(ROW_CHAIN, nc) == 0 and nl % KEY_BLOCK == 0 and p_len % LANES == 0

    cmat = jnp.zeros((MOD_ROWS, d), F32).at[0].set(c_ctx).at[1:1 + bl].set(c)
    mods = _modulation(cmat, w_mod, b_mod).reshape(depth, MOD_ROWS, 1, 6 * d)

    cos_t, sin_t = _rope_tables(nl)
    ut = np.arange(CUMSUM_BLOCK)
    consts = {"bdq": _block_diag_ones(Q_W), "bdk": _block_diag_ones(KV_W), "cos": cos_t, "sin": sin_t,
              "ut": jnp.asarray(ut[:, None] <= ut[None, :], dtype=BF16)}

    kc_rep = jnp.tile(cache_k.transpose(0, 1, 3, 2, 4), (1, 1, 1, 1, GROUP)).astype(BF16)
    vc_t = cache_v.transpose(0, 1, 3, 4, 2).reshape(bl, depth, KV_W, p_len).astype(BF16)

    ctx_row = lambda bi: 0
    lat_row = lambda bi: bi + 1
    ctx_group = min(CTX_GROUP, bc)

    expert_w = [w.reshape(depth, N_EXPERTS * w.shape[2], w.shape[3]) for w in (w_gate, w_up, w_down)]
    state = {"ctx": (x_prompt, None), "lat": (x_sample, None)}
    new_k, new_v = [], []
    for l in range(depth):
        lw = {
            "norm1": norm1[l].reshape(1, d), "norm2": norm2[l].reshape(1, d),
            "w_in": w_in[l].astype(BF16),
            "gq": jnp.tile(q_norm[l], N_HEADS).reshape(1, Q_W), "gk": jnp.tile(k_norm[l], N_KV_HEADS).reshape(1, KV_W),
            "conv_w": jnp.zeros((SUBLANES, CONV_W), F32).at[0:3].set(conv_w[l]),
            "w_conv_out": w_conv_out[l].astype(BF16), "w_attn_out": w_attn_out[l].astype(BF16),
            "w_o": w_o[l].astype(BF16),
            "w_router": jnp.zeros((d, LANES), F32).at[:, 0:N_EXPERTS].set(w_router[l]),
        }
        mod_l = mods[l]
        modp = mods[l - 1] if l else None
        xa, xb = state["lat"]
        lat_res = _premix(xa, xb, modp if xb is not None else None, mod_l, lat_row, lw, consts, True,
                          cast=[(w, l) for w in expert_w])
        for key, w16, w32 in zip(("w_gate", "w_up", "w_down"), lat_res[:3], (w_gate, w_up, w_down)):
            lw[key] = w16.reshape(w32.shape[1:])
        for name in ("ctx", "lat"):
            is_lat = name == "lat"
            row_fn = lat_row if is_lat else ctx_row
            xa, xb = state[name]
            if is_lat:
                res = lat_res[3:]
            else:
                res = _premix(xa, xb, modp if xb is not None else None, mod_l, row_fn, lw, consts, is_lat)
            q, krep, vt = res[0:3]
            pos = 3
            if not is_lat:
                new_k.append(res[3])
                new_v.append(res[4])
                pos = 5
            u, gb, sga, sgat = res[pos:pos + 4]
            x_cur = res[pos + 4] if xb is not None else xa
            if is_lat:
                attn = _attention(q, krep, vt, kc_rep[:, l], vc_t[:, l])
            else:
                attn = _attention(q, krep, vt)
            x1, hx, afft = _postmix(attn, u, gb, sga, sgat, x_cur, mod_l, row_fn, lw)
            moe = _expert_choice(hx, afft, lw, consts, group=1 if is_lat else ctx_group)
            state[name] = (x1, moe)

    g = final_norm.reshape(1, d)
    y_prompt = _final_norm(*state["ctx"], mods[depth - 1], ctx_row, g)
    y_sample = _final_norm(*state["lat"], mods[depth - 1], lat_row, g)
    shape_kv = (bc, depth, nc, N_KV_HEADS, HEAD_DIM)
    new_cache_k = jnp.stack(new_k, axis=1).reshape(shape_kv)
    new_cache_v = jnp.stack(new_v, axis=1).reshape(shape_kv)
    return (y_prompt, y_sample, new_cache_k, new_cache_v)
```

```python
import functools

import jax
import jax.numpy as jnp
import numpy as np
from jax import lax
from jax.experimental import pallas as pl
from jax.experimental.pallas import tpu as pltpu

F32 = jnp.float32
BF16 = jnp.bfloat16
I32 = jnp.int32
U32 = jnp.uint32

D_MODEL = 1024
N_HEADS = 16
N_KV_HEADS = 4
HEAD_DIM = 64
GROUP = N_HEADS // N_KV_HEADS
Q_W = N_HEADS * HEAD_DIM
KV_W = N_KV_HEADS * HEAD_DIM
CONV_W = D_MODEL // 2
IN_W = Q_W + 2 * KV_W + 3 * CONV_W + 2 * D_MODEL
N_EXPERTS = 16
EC_FACTOR = 2
D_EXPERT = 1024
GRID_W = 64
AXIS_DIM = HEAD_DIM // 2
ROPE_THETA = 10000.0
EPS = 1e-6
MOD_ROWS = 16

LANES = 128
SUBLANES = 8
VMEM_LIMIT_BYTES = 56 * 1024 * 1024

MIX_TILE = 512
ROW_CHAIN = 256
FINAL_TILE = 1024
KEY_BLOCK = 1024
QUERY_TILE = 512
CUMSUM_BLOCK = 256
RANK_BLOCK = 128
TOPK_ROWS = 512 * 1024
CTX_GROUP = 16
ONES_ROWS = 16
GATHER_W = D_MODEL // 2 + LANES
LOG2_E = 1.4426950408889634


def _cparams(sem):
    return pltpu.CompilerParams(dimension_semantics=sem, vmem_limit_bytes=VMEM_LIMIT_BYTES)


def _mm(a, b):
    return jnp.dot(a, b, preferred_element_type=F32)


def _mm_tn(a_t, b):
    return lax.dot_general(a_t, b, (((0,), (0,)), ((), ())), preferred_element_type=F32)


def _const_spec(shape, single=False):
    nd = len(shape)
    return pl.BlockSpec(shape, lambda *_: (0,) * nd, pipeline_mode=pl.Buffered(1) if single else None)


def _mod_body(c_ref, w_ref, b_ref, o_ref):
    c = c_ref[...]
    s = c * jax.nn.sigmoid(c)
    o_ref[0] = _mm(s.astype(BF16), w_ref[0].astype(BF16)) + b_ref[0]


def _modulation(cmat, w_mod, b_mod):
    depth = w_mod.shape[0]
    tn = 1536
    return pl.pallas_call(
        _mod_body,
        out_shape=jax.ShapeDtypeStruct((depth, MOD_ROWS, 6 * D_MODEL), F32),
        grid=(depth, 6 * D_MODEL // tn),
        in_specs=[
            pl.BlockSpec((MOD_ROWS, D_MODEL), lambda l, j: (0, 0)),
            pl.BlockSpec((1, D_MODEL, tn), lambda l, j: (l, 0, j)),
            pl.BlockSpec((1, 1, tn), lambda l, j: (l, 0, j)),
        ],
        out_specs=pl.BlockSpec((1, MOD_ROWS, tn), lambda l, j: (l, 0, j)),
        compiler_params=_cparams(("arbitrary", "arbitrary")),
        name="modulation",
    )(cmat, w_mod, b_mod.reshape(depth, 1, 6 * D_MODEL))


def _rms(x):
    return x * lax.rsqrt(jnp.mean(x * x, axis=-1, keepdims=True) + EPS)


def _premix_body(*refs, is_lat, has_prev, tm, n_cast):
    it = iter(refs)
    xa_ref = next(it)
    if has_prev:
        xb_ref = next(it)
        modp_ref = next(it)
    mod_ref = next(it)
    n1_ref = next(it)
    win_ref = next(it)
    gq_ref = next(it)
    gk_ref = next(it)
    bdq_ref = next(it)
    bdk_ref = next(it)
    if is_lat:
        cos_ref = next(it)
        sin_ref = next(it)
    cast_in = [next(it) for _ in range(n_cast)]
    cast_out = [next(it) for _ in range(n_cast)]
    for src, dst in zip(cast_in, cast_out):
        dst[...] = src[0].astype(BF16)
    q_ref = next(it)
    krep_ref = next(it)
    vt_ref = next(it)
    if not is_lat:
        kc_ref = next(it)
        vc_ref = next(it)
    u_ref = next(it)
    gb_ref = next(it)
    sga_ref = next(it)
    sgat_ref = next(it)
    if has_prev:
        xn_ref = next(it)

    d = D_MODEL
    sub = min(ROW_CHAIN, tm)
    lane = lax.broadcasted_iota(I32, (sub, LANES), 1)
    first_half = (lane % AXIS_DIM) < (AXIS_DIM // 2)
    low_head = lane < HEAD_DIM
    sh1 = mod_ref[0, :, 0:d]
    sc1 = mod_ref[0, :, d:2 * d]

    def row_chain(rows):
        x = xa_ref[0, rows, :]
        if has_prev:
            x = x + modp_ref[0, :, 5 * d:6 * d] * xb_ref[0, rows, :]
            xn_ref[0, rows, :] = x
        h = (_rms(x) * n1_ref[...]) * (1.0 + sc1) + sh1
        hb = h.astype(BF16)

        def proj(lo, hi):
            return _mm(hb, win_ref[:, lo:hi])

        def rope(chunk):
            if not is_lat:
                return chunk
            partner = jnp.where(first_half, pltpu.roll(chunk, LANES - AXIS_DIM // 2, 1),
                                pltpu.roll(chunk, AXIS_DIM // 2, 1))
            return chunk * cos_ref[rows, :] + partner * sin_ref[rows, :]

        pq = proj(0, Q_W)
        ssq = _mm((pq * pq).astype(BF16), bdq_ref[...])
        qn = pq * lax.rsqrt(ssq * (1.0 / HEAD_DIM) + EPS) * gq_ref[...]
        for c in range(Q_W // LANES):
            qc = rope(qn[:, c * LANES:(c + 1) * LANES])
            q_ref[0, c * LANES:(c + 1) * LANES, rows] = (qc * (HEAD_DIM ** -0.5 * LOG2_E)).T.astype(BF16)

        pk = proj(Q_W, Q_W + KV_W)
        ssk = _mm((pk * pk).astype(BF16), bdk_ref[...])
        kn = pk * lax.rsqrt(ssk * (1.0 / HEAD_DIM) + EPS) * gk_ref[...]
        if not is_lat:
            kc_ref[0, rows, :] = kn
        for c in range(KV_W // LANES):
            kc = rope(kn[:, c * LANES:(c + 1) * LANES])
            rolled = pltpu.roll(kc, HEAD_DIM, 1)
            even = jnp.where(low_head, kc, rolled).astype(BF16)
            odd = jnp.where(low_head, rolled, kc).astype(BF16)
            for s in range(KV_W // LANES):
                krep_ref[0, 2 * c, rows, s * LANES:(s + 1) * LANES] = even
                krep_ref[0, 2 * c + 1, rows, s * LANES:(s + 1) * LANES] = odd

        pv = proj(Q_W + KV_W, Q_W + 2 * KV_W)
        if not is_lat:
            vc_ref[0, rows, :] = pv
        vt_ref[0, 0, :, rows] = pv.T.astype(BF16)

        o = Q_W + 2 * KV_W
        gb_ref[0, rows, :] = proj(o, o + CONV_W).astype(BF16)
        u_ref[0, rows, :] = proj(o + CONV_W, o + 2 * CONV_W) * proj(o + 2 * CONV_W, o + 3 * CONV_W)
        o = o + 3 * CONV_W
        sga_ref[0, rows, :] = jax.nn.sigmoid(proj(o, o + d)).astype(BF16)
        sgat_ref[0, rows, :] = jax.nn.sigmoid(proj(o + d, o + 2 * d)).astype(BF16)

    for c in range(tm // sub):
        row_chain(slice(c * sub, (c + 1) * sub))


def _premix(xa, xb, modp, mod_l, row_of_batch, lw, consts, is_lat, cast=()):
    b, n, d = xa.shape
    tm = min(MIX_TILE, n)
    tk = min(KEY_BLOCK, n)
    has_prev = xb is not None
    sub = tk // tm
    steps = n // tm
    tok_spec = lambda w: pl.BlockSpec((1, tm, w), lambda bi, i: (bi, i, 0))
    mod_spec = pl.BlockSpec((1, 1, 6 * d), lambda bi, i: (row_of_batch(bi), 0, 0))

    ins, specs = [xa], [tok_spec(d)]
    if has_prev:
        ins += [xb, modp]
        specs += [tok_spec(d), mod_spec]
    ins += [mod_l, lw["norm1"], lw["w_in"], lw["gq"], lw["gk"], consts["bdq"], consts["bdk"]]
    specs += [mod_spec, _const_spec((1, d)), _const_spec((d, IN_W), single=True), _const_spec((1, Q_W)),
              _const_spec((1, KV_W)), _const_spec((Q_W, Q_W), single=True), _const_spec((KV_W, KV_W))]
    if is_lat:
        ins += [consts["cos"], consts["sin"]]
        specs += [pl.BlockSpec((tm, LANES), lambda bi, i: (i, 0))] * 2

    outs, ospecs = [], []
    for w, layer in cast:
        _, wrows, wcols = w.shape
        slab = wrows // (b * steps)
        assert slab * b * steps == wrows and slab % SUBLANES == 0
        ins.append(w)
        specs.append(pl.BlockSpec((1, slab, wcols), lambda bi, i, layer=layer: (layer, bi * steps + i, 0)))
        outs.append(jax.ShapeDtypeStruct((wrows, wcols), BF16))
        ospecs.append(pl.BlockSpec((slab, wcols), lambda bi, i: (bi * steps + i, 0)))
    outs += [jax.ShapeDtypeStruct((b, Q_W, n), BF16),
             jax.ShapeDtypeStruct((b, N_KV_HEADS, n, KV_W), BF16),
             jax.ShapeDtypeStruct((b, n // tk, KV_W, tk), BF16)]
    ospecs += [pl.BlockSpec((1, Q_W, tm), lambda bi, i: (bi, 0, i)),
               pl.BlockSpec((1, N_KV_HEADS, tm, KV_W), lambda bi, i: (bi, 0, i, 0)),
               pl.BlockSpec((1, 1, KV_W, tm), lambda bi, i: (bi, i // sub, 0, i % sub))]
    if not is_lat:
        outs += [jax.ShapeDtypeStruct((b, n, KV_W), F32)] * 2
        ospecs += [tok_spec(KV_W)] * 2
    outs += [jax.ShapeDtypeStruct((b, n, CONV_W), F32), jax.ShapeDtypeStruct((b, n, CONV_W), BF16),
             jax.ShapeDtypeStruct((b, n, d), BF16), jax.ShapeDtypeStruct((b, n, d), BF16)]
    ospecs += [tok_spec(CONV_W), tok_spec(CONV_W), tok_spec(d), tok_spec(d)]
    if has_prev:
        outs.append(jax.ShapeDtypeStruct((b, n, d), F32))
        ospecs.append(tok_spec(d))

    res = pl.pallas_call(
        functools.partial(_premix_body, is_lat=is_lat, has_prev=has_prev, tm=tm, n_cast=len(cast)),
        out_shape=outs, grid=(b, n // tm), in_specs=specs, out_specs=ospecs,
        compiler_params=_cparams(("arbitrary", "arbitrary")),
        name="premix_lat" if is_lat else "premix_ctx",
    )(*ins)
    return list(res)


def _attn_body(*refs, p_len, n, tq, tk, hps):
    it = iter(refs)
    q_ref = next(it)
    if p_len:
        kc_ref = next(it)
        vc_ref = next(it)
    k_ref = next(it)
    v_ref = next(it)
    o_ref = next(it)
    qbd_ref = next(it)
    s_ref = next(it)
    m_ref = next(it)
    acc_ref = next(it)

    m_ref[...] = jnp.full(m_ref.shape, -jnp.inf, F32)
    acc_ref[...] = jnp.zeros(acc_ref.shape, F32)
    nb = n // tk

    def one_kv_head(hh):
        base = hh * GROUP
        vrows = slice(hh * HEAD_DIM, (hh + 1) * HEAD_DIM)
        lanes = slice(hh * KV_W, (hh + 1) * KV_W)

        for g in range(GROUP):
            band = q_ref[0, hh * KV_W + g * HEAD_DIM:hh * KV_W + (g + 1) * HEAD_DIM, :]
            pieces = [jnp.zeros((g * HEAD_DIM, tq), BF16), band, jnp.zeros(((GROUP - 1 - g) * HEAD_DIM, tq), BF16)]
            qbd_ref[base + g] = jnp.concatenate([piece for piece in pieces if piece.shape[0]], axis=0)

        def score(parts, g):
            off, m_blk = 0, None
            for get_k, _, size in parts:
                s = _mm(get_k(), qbd_ref[base + g])
                s_ref[base + g, off:off + size, :] = s
                m_part = jnp.max(s, axis=0, keepdims=True)
                m_blk = m_part if m_blk is None else jnp.maximum(m_blk, m_part)
                off += size
            return m_blk

        def softmax_pv(parts, g, m_blk):
            size = sum(part[2] for part in parts)
            m_prev = m_ref[base + g]
            m_new = jnp.maximum(m_prev, m_blk)
            alpha = jnp.exp2(m_prev - m_new)
            p = jnp.exp2(s_ref[base + g, 0:size, :] - m_new).astype(BF16)
            v_ext = jnp.concatenate([jnp.concatenate([get_v() for _, get_v, _ in parts], axis=1),
                                     jnp.ones((ONES_ROWS, size), BF16)], axis=0)
            acc_ref[base + g] = alpha * acc_ref[base + g] + _mm(v_ext, p)
            m_ref[base + g] = m_new

        def key_block(parts, m_blk, next_parts):
            for g in range(GROUP):
                if g + 1 < GROUP:
                    m_next = score(parts, g + 1)
                elif next_parts is not None:
                    m_next = score(next_parts, 0)
                else:
                    m_next = None
                softmax_pv(parts, g, m_blk)
                m_blk = m_next
            return m_blk

        def block_parts(i):
            if isinstance(i, int):
                parts = [(lambda: k_ref[0, hh, i * tk:(i + 1) * tk, :], lambda: v_ref[0, i, vrows, :], tk)]
                if p_len and i == 0:
                    parts.append((lambda: kc_ref[0, hh], lambda: vc_ref[0, vrows, :], p_len))
                return parts
            return [(lambda: k_ref[0, hh, pl.ds(pl.multiple_of(i * tk, tk), tk), :], lambda: v_ref[0, i, vrows, :], tk)]

        m_blk = score(block_parts(0), 0)
        if nb > 1:
            m_blk = key_block(block_parts(0), m_blk, block_parts(1))
            m_blk = lax.fori_loop(1, nb - 1, lambda i, m: key_block(block_parts(i), m, block_parts(i + 1)), m_blk)
        key_block(block_parts(nb - 1), m_blk, None)

        outs = []
        for g in range(GROUP):
            acc = acc_ref[base + g]
            outs.append(acc[0:HEAD_DIM, :] * (1.0 / acc[HEAD_DIM:HEAD_DIM + 1, :]))
        o_ref[0, lanes, :] = jnp.concatenate(outs, axis=0).astype(BF16)

    for hh in range(hps):
        one_kv_head(hh)


def _attention(q, krep, vt, kc_rep=None, vc_t=None):
    b, _, n = q.shape
    tk = vt.shape[-1]
    tq = min(QUERY_TILE, n)
    p_len = 0 if kc_rep is None else kc_rep.shape[2]
    hps = N_KV_HEADS if n == tk else 1
    ins, specs = [q], [pl.BlockSpec((1, hps * KV_W, tq), lambda bi, h, i: (bi, h, i))]
    if p_len:
        ins += [kc_rep, vc_t]
        specs += [pl.BlockSpec((1, hps, p_len, KV_W), lambda bi, h, i: (bi, h, 0, 0)),
                  pl.BlockSpec((1, hps * HEAD_DIM, p_len), lambda bi, h, i: (bi, h, 0))]
    ins += [krep, vt]
    specs += [pl.BlockSpec((1, hps, n, KV_W), lambda bi, h, i: (bi, h, 0, 0)),
              pl.BlockSpec((1, n // tk, hps * HEAD_DIM, tk), lambda bi, h, i: (bi, 0, h, 0))]
    return pl.pallas_call(
        functools.partial(_attn_body, p_len=p_len, n=n, tq=tq, tk=tk, hps=hps),
        out_shape=jax.ShapeDtypeStruct((b, Q_W, n), BF16),
        grid=(b, N_KV_HEADS // hps, n // tq),
        in_specs=specs,
        out_specs=pl.BlockSpec((1, hps * KV_W, tq), lambda bi, h, i: (bi, h, i)),
        scratch_shapes=[pltpu.VMEM((hps * GROUP, KV_W, tq), BF16),
                        pltpu.VMEM((hps * GROUP, tk + p_len, tq), F32),
                        pltpu.VMEM((hps * GROUP, 1, tq), F32),
                        pltpu.VMEM((hps * GROUP, HEAD_DIM + ONES_ROWS, tq), F32)],
        compiler_params=_cparams(("arbitrary", "arbitrary", "arbitrary")),
        name="attention_lat" if p_len else "attention_ctx",
    )(*ins)


def _split_bf16(x):
    hi = x.astype(BF16)
    lo = (x - hi.astype(F32)).astype(BF16)
    return hi, lo


def _postmix_body(attn_ref, u_ref, up_ref, un_ref, gb_ref, sga_ref, sgat_ref, x_ref, mod_ref, n2_ref, cw_ref,
                  wa_ref, wc_ref, wo_ref, wr_ref, x1_ref, hx_ref, afft_ref, *, tm):
    d = D_MODEL
    i = pl.program_id(1)
    last = pl.num_programs(1) - 1
    u = u_ref[0]
    row = lax.broadcasted_iota(I32, (tm, CONV_W), 0)
    prev_row = jnp.where(i > 0, up_ref[0, SUBLANES - 1:SUBLANES, :], 0.0)
    next_row = jnp.where(i < last, un_ref[0, 0:1, :], 0.0)
    u_m1 = jnp.where(row == 0, prev_row, pltpu.roll(u, 1, 0))
    u_p1 = jnp.where(row == tm - 1, next_row, pltpu.roll(u, tm - 1, 0))
    conv = cw_ref[0:1, :] * u_m1 + cw_ref[1:2, :] * u + cw_ref[2:3, :] * u_p1
    cv = (gb_ref[0].astype(F32) * conv).astype(BF16)
    r_hi, r_lo = _split_bf16(wr_ref[...])
    half = d // 2
    sub = min(ROW_CHAIN, tm)
    lane = lax.broadcasted_iota(I32, (sub, LANES), 1)
    valid = lane < N_EXPERTS

    for c in range(tm // sub):
        rows = slice(c * sub, (c + 1) * sub)
        conv_o = _mm(cv[rows], wc_ref[...])
        attn_o = _mm_tn(attn_ref[0, :, rows], wa_ref[...])
        merged = sga_ref[0, rows, :].astype(F32) * conv_o + sgat_ref[0, rows, :].astype(F32) * attn_o
        mix = _mm(merged.astype(BF16), wo_ref[...])
        x1 = x_ref[0, rows, :] + mod_ref[0, :, 2 * d:3 * d] * mix
        x1_ref[0, rows, :] = x1
        h2 = (_rms(x1) * n2_ref[...]) * (1.0 + mod_ref[0, :, 4 * d:5 * d]) + mod_ref[0, :, 3 * d:4 * d]

        h_hi, h_lo = _split_bf16(h2)
        w_lo = lax.bitcast_convert_type(h_hi[:, :half].astype(F32), U32) >> 16
        w_hi = lax.bitcast_convert_type(h_hi[:, half:].astype(F32), U32) & jnp.uint32(0xFFFF0000)
        hx_ref[0, rows, 0:half] = w_lo | w_hi

        logits = (_mm(h_hi, r_hi) + _mm(h_hi, r_lo)
                  + _mm(h_lo, r_hi))
        logits = jnp.where(valid, logits, -jnp.inf)
        e = jnp.exp(logits - jnp.max(logits, axis=-1, keepdims=True))
        e = jnp.where(valid, e, 0.0)
        aff = e / jnp.sum(e, axis=-1, keepdims=True)
        hx_ref[0, rows, half:half + LANES] = lax.bitcast_convert_type(aff, U32)
        afft_ref[0, :, rows] = aff.T[0:N_EXPERTS, :]


def _postmix(attn, u, gb, sga, sgat, x, mod_l, row_of_batch, lw):
    b, n, d = x.shape
    tm = min(MIX_TILE, n)
    nsub = n // SUBLANES
    per = tm // SUBLANES
    tok_spec = lambda w: pl.BlockSpec((1, tm, w), lambda bi, i: (bi, i, 0))
    specs = [pl.BlockSpec((1, Q_W, tm), lambda bi, i: (bi, 0, i)), tok_spec(CONV_W),
             pl.BlockSpec((1, SUBLANES, CONV_W), lambda bi, i: (bi, jnp.maximum(i * per - 1, 0), 0)),
             pl.BlockSpec((1, SUBLANES, CONV_W), lambda bi, i: (bi, jnp.minimum((i + 1) * per, nsub - 1), 0)),
             tok_spec(CONV_W), tok_spec(d), tok_spec(d), tok_spec(d),
             pl.BlockSpec((1, 1, 6 * d), lambda bi, i: (row_of_batch(bi), 0, 0)),
             _const_spec((1, d)), _const_spec((SUBLANES, CONV_W)),
             _const_spec((Q_W, d)), _const_spec((CONV_W, d)), _const_spec((d, d)), _const_spec((d, LANES))]
    outs = [jax.ShapeDtypeStruct((b, n, d), F32), jax.ShapeDtypeStruct((b, n, GATHER_W), U32),
            jax.ShapeDtypeStruct((b, N_EXPERTS, n), F32)]
    ospecs = [tok_spec(d), tok_spec(GATHER_W),
              pl.BlockSpec((1, N_EXPERTS, tm), lambda bi, i: (bi, 0, i))]
    return pl.pallas_call(
        functools.partial(_postmix_body, tm=tm),
        out_shape=outs, grid=(b, n // tm), in_specs=specs, out_specs=ospecs,
        compiler_params=_cparams(("arbitrary", "arbitrary")),
        name="postmix",
    )(attn, u, u, u, gb, sga, sgat, x, mod_l, lw["norm2"], lw["conv_w"], lw["w_attn_out"], lw["w_conv_out"],
      lw["w_o"], lw["w_router"])


def _topk_body(aff_ref, ut_ref, idx_ref, sel_ref, pos_ref, *, gb, n, cap):
    ne = N_EXPERTS
    rows_all = gb * ne
    a = aff_ref[...].reshape(rows_all, n)
    bits = lax.bitcast_convert_type(a, I32)

    def count(mask):
        return jnp.sum(jnp.where(mask, 1.0, 0.0), axis=1, keepdims=True)

    def thr_step(t, thr):
        cand = thr | (jnp.int32(1) << (30 - t))
        return jnp.where(count(bits >= cand) >= cap, cand, thr)

    thr = lax.fori_loop(0, 31, thr_step, jnp.zeros((rows_all, 1), I32))
    gt = bits > thr
    eq = bits == thr
    need = cap - count(gt)
    tok = lax.broadcasted_iota(I32, (rows_all, n), 1)
    nbits = int(np.log2(n)) + 1

    def tie_step(t, bound):
        cand = bound + (jnp.int32(1) << (nbits - 1 - t))
        ok = (cand <= n) & (count(eq & (tok < cand)) <= need)
        return jnp.where(ok, cand, bound)

    bound = lax.fori_loop(0, nbits, tie_step, jnp.zeros((rows_all, 1), I32))
    sel_ref[...] = jnp.where(gt | (eq & (tok < bound)), 1.0, 0.0)

    cb = min(CUMSUM_BLOCK, n)
    nblk = n // cb
    nrb = max(cap // RANK_BLOCK, 1)
    width = min(cap, RANK_BLOCK)
    tiles = RANK_BLOCK // SUBLANES
    lane = lax.broadcasted_iota(I32, (RANK_BLOCK, LANES), 1)
    rank0 = (lax.broadcasted_iota(I32, (tiles, SUBLANES, LANES), 0) * SUBLANES
             + lax.broadcasted_iota(I32, (tiles, SUBLANES, LANES), 1)).astype(F32)

    def batch_step(bi, carry):
        sel = sel_ref[pl.ds(pl.multiple_of(bi * ne, ne), ne), :]
        off = jnp.zeros((ne, 1), F32)
        for k in range(nblk):
            blk = _mm(sel[:, k * cb:(k + 1) * cb].astype(BF16), ut_ref[...])
            pos = blk + off
            for ex in range(ne):
                pos_ref[ex, :, k * cb:(k + 1) * cb] = jnp.broadcast_to(pos[ex:ex + 1, :], (SUBLANES, cb))
            off = off + blk[:, cb - 1:cb]

        def expert_step(ei, cols):
            for rb in range(nrb):
                rank = rank0 + float(rb * RANK_BLOCK)
                cnt = jnp.zeros((tiles, SUBLANES, LANES), F32)
                for lc in range(n // LANES):
                    prow = pos_ref[ei, :, lc * LANES:(lc + 1) * LANES]
                    cnt = cnt + jnp.where(prow[None] <= rank, 1.0, 0.0)
                col = jnp.sum(cnt.reshape(RANK_BLOCK, LANES), axis=1, keepdims=True)
                cols = jnp.where(lane == ei * nrb + rb, col, cols)
            return cols

        cols = lax.fori_loop(0, ne, expert_step, jnp.zeros((RANK_BLOCK, LANES), F32))
        rows = cols.T
        idx_ref[bi] = rows[0:ne * nrb, 0:width].astype(I32)
        return carry

    lax.fori_loop(0, gb, batch_step, 0)


def _topk(afft, consts):
    b, ne, n = afft.shape
    cap = max(1, EC_FACTOR * n // N_EXPERTS)
    nrb = max(cap // RANK_BLOCK, 1)
    width = min(cap, RANK_BLOCK)
    cb = min(CUMSUM_BLOCK, n)
    gb = min(b, max(1, TOPK_ROWS // (ne * n)))
    assert b % gb == 0
    idx = pl.pallas_call(
        functools.partial(_topk_body, gb=gb, n=n, cap=cap),
        out_shape=jax.ShapeDtypeStruct((b, ne * nrb, width), I32),
        grid=(b // gb,),
        in_specs=[pl.BlockSpec((gb, ne, n), lambda bi: (bi, 0, 0)), _const_spec((cb, cb))],
        out_specs=pl.BlockSpec((gb, ne * nrb, width), lambda bi: (bi, 0, 0)),
        scratch_shapes=[pltpu.VMEM((gb * ne, n), F32), pltpu.VMEM((ne, SUBLANES, n), F32)],
        compiler_params=_cparams(("arbitrary",)),
        name="topk",
    )(afft, consts["ut"][:cb, :cb])
    return idx.reshape(b, ne, cap)


def _ffn_body(idx_ref, idx_next_ref, hx_ref, wg_ref, wu_ref, wd_ref, y_ref, xs_ref, *, rows):
    e = pl.program_id(1)

    def gather_row(src_ref, slot, j, k):
        t = src_ref[0, 0, j * SUBLANES + k]
        xs_ref[slot, j, pl.ds(k, 1), :] = hx_ref[0, pl.ds(t, 1), :]

    @pl.when(e == 0)
    def _gather_first_expert():
        def body(j, carry):
            for k in range(SUBLANES):
                gather_row(idx_ref, 0, j, k)
            return carry
        lax.fori_loop(0, rows // SUBLANES, body, 0)

    slot = lax.rem(e, 2)
    half = D_MODEL // 2
    gathered = xs_ref[slot].reshape(rows, GATHER_W)
    for j in range(rows // SUBLANES):
        for k in range(SUBLANES):
            gather_row(idx_next_ref, 1 - slot, j, k)
    sub = min(ROW_CHAIN, rows)
    lane = lax.broadcasted_iota(I32, (sub, LANES), 1)
    for c in range(rows // sub):
        part = gathered[c * sub:(c + 1) * sub, :]
        words = part[:, 0:half]
        aff = lax.bitcast_convert_type(part[:, half:half + LANES], F32)
        x_lo = lax.bitcast_convert_type(words << 16, F32).astype(BF16)
        x_hi = lax.bitcast_convert_type(words & jnp.uint32(0xFFFF0000), F32).astype(BF16)

        def up_proj(w_ref):
            return (_mm(x_lo, w_ref[0, 0:half, :])
                    + _mm(x_hi, w_ref[0, half:, :]))

        gate = up_proj(wg_ref)
        hidden = ((gate * jax.nn.sigmoid(gate)) * up_proj(wu_ref)).astype(BF16)
        y = _mm(hidden, wd_ref[0])
        val = jnp.sum(jnp.where(lane == e, aff, 0.0), axis=1, keepdims=True)
        tile0 = c * (sub // SUBLANES)
        y_ref[0, 0, tile0:tile0 + sub // SUBLANES] = (y * val).reshape(sub // SUBLANES, SUBLANES, D_MODEL)


def _moe_ffn(idx_steps, hx, lw, group):
    ng, ne, rows = idx_steps.shape
    gtok = hx.shape[1] * group
    hx = hx.reshape(ng, gtok, GATHER_W)
    idx_flat = idx_steps.reshape(ng * ne, 1, rows)
    w_spec = lambda r, c: pl.BlockSpec((1, r, c), lambda g, e: (e, 0, 0))
    idx_spec = lambda step: pl.BlockSpec((1, 1, rows), lambda g, e: (g * ne + jnp.minimum(e + step, ne - 1), 0, 0),
                                         memory_space=pltpu.SMEM)
    return pl.pallas_call(
        functools.partial(_ffn_body, rows=rows),
        out_shape=jax.ShapeDtypeStruct((ng, ne, rows // SUBLANES, SUBLANES, D_MODEL), F32),
        grid=(ng, ne),
        in_specs=[idx_spec(0), idx_spec(1), pl.BlockSpec((1, gtok, GATHER_W), lambda g, e: (g, 0, 0)),
                  w_spec(D_MODEL, D_EXPERT), w_spec(D_MODEL, D_EXPERT), w_spec(D_EXPERT, D_MODEL)],
        out_specs=pl.BlockSpec((1, 1, rows // SUBLANES, SUBLANES, D_MODEL), lambda g, e: (g, e, 0, 0, 0)),
        scratch_shapes=[pltpu.VMEM((2, rows // SUBLANES, SUBLANES, GATHER_W), U32)],
        compiler_params=_cparams(("arbitrary", "arbitrary")),
        name="moe_ffn",
    )(idx_flat, idx_flat, hx, lw["w_gate"], lw["w_up"], lw["w_down"])


def _combine_body(idx_ref, y_ref, o_ref, *, rows):
    @pl.when(pl.program_id(1) == 0)
    def _zero():
        o_ref[...] = jnp.zeros(o_ref.shape, F32)

    def scatter(j, carry):
        toks = [idx_ref[0, 0, j * SUBLANES + k] for k in range(SUBLANES)]
        vals = [o_ref[0, pl.ds(toks[k], 1), :] + y_ref[0, 0, j, pl.ds(k, 1), :] for k in range(SUBLANES)]
        for k in range(SUBLANES):
            o_ref[0, pl.ds(toks[k], 1), :] = vals[k]
        return carry

    lax.fori_loop(0, rows // SUBLANES, scatter, 0)


def _combine(idx_steps, ysel, gtok):
    ng, ne, rows = idx_steps.shape
    return pl.pallas_call(
        functools.partial(_combine_body, rows=rows),
        out_shape=jax.ShapeDtypeStruct((ng, gtok, D_MODEL), F32),
        grid=(ng, ne),
        in_specs=[pl.BlockSpec((1, 1, rows), lambda g, e: (g * ne + e, 0, 0), memory_space=pltpu.SMEM),
                  pl.BlockSpec((1, 1, rows // SUBLANES, SUBLANES, D_MODEL), lambda g, e: (g, e, 0, 0, 0))],
        out_specs=pl.BlockSpec((1, gtok, D_MODEL), lambda g, e: (g, 0, 0)),
        compiler_params=_cparams(("arbitrary", "arbitrary")),
        name="moe_combine",
    )(idx_steps.reshape(ng * ne, 1, rows), ysel)


def _expert_choice(hx, afft, lw, consts, group):
    b, n, _ = hx.shape
    idx = _topk(afft, consts)
    cap = idx.shape[-1]
    ng = b // group
    offs = (jnp.arange(b, dtype=I32) % group * n).reshape(ng, group, 1, 1)
    idx_steps = (idx.reshape(ng, group, N_EXPERTS, cap) + offs).transpose(0, 2, 1, 3).reshape(ng, N_EXPERTS, group * cap)
    ysel = _moe_ffn(idx_steps, hx, lw, group)
    moe = _combine(idx_steps, ysel, group * n)
    return moe.reshape(b, n, D_MODEL)


def _final_body(x1_ref, moe_ref, mod_ref, g_ref, o_ref):
    d = D_MODEL
    x = x1_ref[0] + mod_ref[0, :, 5 * d:6 * d] * moe_ref[0]
    o_ref[0] = _rms(x) * g_ref[...]


def _final_norm(x1, moe, mod_l, row_of_batch, g):
    b, n, d = x1.shape
    tm = min(FINAL_TILE, n)
    tok = pl.BlockSpec((1, tm, d), lambda bi, i: (bi, i, 0))
    return pl.pallas_call(
        _final_body,
        out_shape=jax.ShapeDtypeStruct((b, n, d), F32),
        grid=(b, n // tm),
        in_specs=[tok, tok, pl.BlockSpec((1, 1, 6 * d), lambda bi, i: (row_of_batch(bi), 0, 0)), _const_spec((1, d))],
        out_specs=tok,
        compiler_params=_cparams(("arbitrary", "arbitrary")),
        name="final_norm",
    )(x1, moe, mod_l, g)


def _rope_tables(n_lat):
    half = AXIS_DIM // 2
    inv_freq = ROPE_THETA ** (-jnp.arange(half, dtype=F32) / half)
    t = jnp.arange(n_lat, dtype=jnp.int32)
    row = (t // GRID_W).astype(F32)
    col = (t % GRID_W).astype(F32)
    lane = np.arange(LANES)
    dim = lane % HEAD_DIM
    use_col = jnp.asarray(dim >= AXIS_DIM)
    freq = inv_freq[jnp.asarray(dim % half)]
    pos = jnp.where(use_col[None, :], col[:, None], row[:, None])
    ang = pos * freq[None, :]
    sign = jnp.asarray(np.where(dim % AXIS_DIM < half, -1.0, 1.0), dtype=F32)
    return jnp.cos(ang), jnp.sin(ang) * sign[None, :]


def _block_diag_ones(width):
    seg = np.arange(width) // HEAD_DIM
    return jnp.asarray(seg[:, None] == seg[None, :], dtype=BF16)


def kernel(x_prompt, x_sample, cache_k, cache_v, c, c_ctx, w_mod, b_mod, norm1, norm2, w_in, q_norm, k_norm,
           conv_w, w_conv_out, w_attn_out, w_o, w_router, w_gate, w_up, w_down, final_norm):
    depth = w_mod.shape[0]
    bc, nc, d = x_prompt.shape
    bl, nl, _ = x_sample.shape
    p_len = cache_k.shape[2]
    assert d == D_MODEL and bl + 1 <= MOD_ROWS
    assert nc % min(ROW_CHAIN, nc) == 0 and nl % KEY_BLOCK == 0 and p_len % LANES == 0

    cmat = jnp.zeros((MOD_ROWS, d), F32).at[0].set(c_ctx).at[1:1 + bl].set(c)
    mods = _modulation(cmat, w_mod, b_mod).reshape(depth, MOD_ROWS, 1, 6 * d)

    cos_t, sin_t = _rope_tables(nl)
    ut = np.arange(CUMSUM_BLOCK)
    consts = {"bdq": _block_diag_ones(Q_W), "bdk": _block_diag_ones(KV_W), "cos": cos_t, "sin": sin_t,
              "ut": jnp.asarray(ut[:, None] <= ut[None, :], dtype=BF16)}

    kc_rep = jnp.tile(cache_k.transpose(0, 1, 3, 2, 4), (1, 1, 1, 1, GROUP)).astype(BF16)
    vc_t = cache_v.transpose(0, 1, 3, 4, 2).reshape(bl, depth, KV_W, p_len).astype(BF16)

    ctx_row = lambda bi: 0
    lat_row = lambda bi: bi + 1
    ctx_group = min(CTX_GROUP, bc)

    expert_w = [w.reshape(depth, N_EXPERTS * w.shape[2], w.shape[3]) for w in (w_gate, w_up, w_down)]
    state = {"ctx": (x_prompt, None), "lat": (x_sample, None)}
    new_k, new_v = [], []
    for l in range(depth):
        lw = {
            "norm1": norm1[l].reshape(1, d), "norm2": norm2[l].reshape(1, d),
            "w_in": w_in[l].astype(BF16),
            "gq": jnp.tile(q_norm[l], N_HEADS).reshape(1, Q_W), "gk": jnp.tile(k_norm[l], N_KV_HEADS).reshape(1, KV_W),
            "conv_w": jnp.zeros((SUBLANES, CONV_W), F32).at[0:3].set(conv_w[l]),
            "w_conv_out": w_conv_out[l].astype(BF16), "w_attn_out": w_attn_out[l].astype(BF16),
            "w_o": w_o[l].astype(BF16),
            "w_router": jnp.zeros((d, LANES), F32).at[:, 0:N_EXPERTS].set(w_router[l]),
        }
        mod_l = mods[l]
        modp = mods[l - 1] if l else None
        xa, xb = state["lat"]
        lat_res = _premix(xa, xb, modp if xb is not None else None, mod_l, lat_row, lw, consts, True,
                          cast=[(w, l) for w in expert_w])
        for key, w16, w32 in zip(("w_gate", "w_up", "w_down"), lat_res[:3], (w_gate, w_up, w_down)):
            lw[key] = w16.reshape(w32.shape[1:])
        for name in ("ctx", "lat"):
            is_lat = name == "lat"
            row_fn = lat_row if is_lat else ctx_row
            xa, xb = state[name]
            if is_lat:
                res = lat_res[3:]
            else:
                res = _premix(xa, xb, modp if xb is not None else None, mod_l, row_fn, lw, consts, is_lat)
            q, krep, vt = res[0:3]
            pos = 3
            if not is_lat:
                new_k.append(res[3])
                new_v.append(res[4])
                pos = 5
            u, gb, sga, sgat = res[pos:pos + 4]
            x_cur = res[pos + 4] if xb is not None else xa
            if is_lat:
                attn = _attention(q, krep, vt, kc_rep[:, l], vc_t[:, l])
            else:
                attn = _attention(q, krep, vt)
            x1, hx, afft = _postmix(attn, u, gb, sga, sgat, x_cur, mod_l, row_fn, lw)
            moe = _expert_choice(hx, afft, lw, consts, group=1 if is_lat else ctx_group)
            state[name] = (x1, moe)

    g = final_norm.reshape(1, d)
    y_prompt = _final_norm(*state["ctx"], mods[depth - 1], ctx_row, g)
    y_sample = _final_norm(*state["lat"], mods[depth - 1], lat_row, g)
    shape_kv = (bc, depth, nc, N_KV_HEADS, HEAD_DIM)
    new_cache_k = jnp.stack(new_k, axis=1).reshape(shape_kv)
    new_cache_v = jnp.stack(new_v, axis=1).reshape(shape_kv)
    return (y_prompt, y_sample, new_cache_k, new_cache_v)
```

```python
import functools

import jax
import jax.numpy as jnp
import numpy as np
from jax import lax
from jax.experimental import pallas as pl
from jax.experimental.pallas import tpu as pltpu

F32 = jnp.float32
BF16 = jnp.bfloat16
I32 = jnp.int32
U32 = jnp.uint32

D_MODEL = 1024
N_HEADS = 16
N_KV_HEADS = 4
HEAD_DIM = 64
GROUP = N_HEADS // N_KV_HEADS
Q_W = N_HEADS * HEAD_DIM
KV_W = N_KV_HEADS * HEAD_DIM
CONV_W = D_MODEL // 2
IN_W = Q_W + 2 * KV_W + 3 * CONV_W + 2 * D_MODEL
N_EXPERTS = 16
EC_FACTOR = 2
D_EXPERT = 1024
GRID_W = 64
AXIS_DIM = HEAD_DIM // 2
ROPE_THETA = 10000.0
EPS = 1e-6
MOD_ROWS = 16

LANES = 128
SUBLANES = 8
VMEM_LIMIT_BYTES = 56 * 1024 * 1024

MIX_TILE = 512
ROW_CHAIN = 256
FINAL_TILE = 1024
KEY_BLOCK = 1024
QUERY_TILE = 512
CUMSUM_BLOCK = 256
RANK_BLOCK = 128
TOPK_ROWS = 512 * 1024
CTX_GROUP = 16
ONES_ROWS = 16
GATHER_W = D_MODEL // 2 + LANES
LOG2_E = 1.4426950408889634


def _cparams(sem):
    return pltpu.CompilerParams(dimension_semantics=sem, vmem_limit_bytes=VMEM_LIMIT_BYTES)


def _mm(a, b):
    return jnp.dot(a, b, preferred_element_type=F32)


def _mm_tn(a_t, b):
    return lax.dot_general(a_t, b, (((0,), (0,)), ((), ())), preferred_element_type=F32)


def _const_spec(shape, single=False):
    nd = len(shape)
    return pl.BlockSpec(shape, lambda *_: (0,) * nd, pipeline_mode=pl.Buffered(1) if single else None)


def _mod_body(c_ref, w_ref, b_ref, o_ref):
    c = c_ref[...]
    s = c * jax.nn.sigmoid(c)
    o_ref[0] = _mm(s.astype(BF16), w_ref[0].astype(BF16)) + b_ref[0]


def _modulation(cmat, w_mod, b_mod):
    depth = w_mod.shape[0]
    tn = 1536
    return pl.pallas_call(
        _mod_body,
        out_shape=jax.ShapeDtypeStruct((depth, MOD_ROWS, 6 * D_MODEL), F32),
        grid=(depth, 6 * D_MODEL // tn),
        in_specs=[
            pl.BlockSpec((MOD_ROWS, D_MODEL), lambda l, j: (0, 0)),
            pl.BlockSpec((1, D_MODEL, tn), lambda l, j: (l, 0, j)),
            pl.BlockSpec((1, 1, tn), lambda l, j: (l, 0, j)),
        ],
        out_specs=pl.BlockSpec((1, MOD_ROWS, tn), lambda l, j: (l, 0, j)),
        compiler_params=_cparams(("arbitrary", "arbitrary")),
        name="modulation",
    )(cmat, w_mod, b_mod.reshape(depth, 1, 6 * D_MODEL))


def _rms(x):
    return x * lax.rsqrt(jnp.mean(x * x, axis=-1, keepdims=True) + EPS)


def _premix_body(*refs, is_lat, has_prev, tm, n_cast):
    it = iter(refs)
    xa_ref = next(it)
    if has_prev:
        xb_ref = next(it)
        modp_ref = next(it)
    mod_ref = next(it)
    n1_ref = next(it)
    win_ref = next(it)
    gq_ref = next(it)
    gk_ref = next(it)
    bdq_ref = next(it)
    bdk_ref = next(it)
    if is_lat:
        cos_ref = next(it)
        sin_ref = next(it)
    cast_in = [next(it) for _ in range(n_cast)]
    cast_out = [next(it) for _ in range(n_cast)]
    for src, dst in zip(cast_in, cast_out):
        dst[...] = src[0].astype(BF16)
    q_ref = next(it)
    krep_ref = next(it)
    vt_ref = next(it)
    if not is_lat:
        kc_ref = next(it)
        vc_ref = next(it)
    u_ref = next(it)
    gb_ref = next(it)
    sga_ref = next(it)
    sgat_ref = next(it)
    if has_prev:
        xn_ref = next(it)

    d = D_MODEL
    sub = min(ROW_CHAIN, tm)
    lane = lax.broadcasted_iota(I32, (sub, LANES), 1)
    first_half = (lane % AXIS_DIM) < (AXIS_DIM // 2)
    low_head = lane < HEAD_DIM
    sh1 = mod_ref[0, :, 0:d]
    sc1 = mod_ref[0, :, d:2 * d]

    def row_chain(rows):
        x = xa_ref[0, rows, :]
        if has_prev:
            x = x + modp_ref[0, :, 5 * d:6 * d] * xb_ref[0, rows, :]
            xn_ref[0, rows, :] = x
        h = (_rms(x) * n1_ref[...]) * (1.0 + sc1) + sh1
        hb = h.astype(BF16)

        def proj(lo, hi):
            return _mm(hb, win_ref[:, lo:hi])

        def rope(chunk):
            if not is_lat:
                return chunk
            partner = jnp.where(first_half, pltpu.roll(chunk, LANES - AXIS_DIM // 2, 1),
                                pltpu.roll(chunk, AXIS_DIM // 2, 1))
            return chunk * cos_ref[rows, :] + partner * sin_ref[rows, :]

        pq = proj(0, Q_W)
        ssq = _mm((pq * pq).astype(BF16), bdq_ref[...])
        qn = pq * lax.rsqrt(ssq * (1.0 / HEAD_DIM) + EPS) * gq_ref[...]
        for c in range(Q_W // LANES):
            qc = rope(qn[:, c * LANES:(c + 1) * LANES])
            q_ref[0, c * LANES:(c + 1) * LANES, rows] = (qc * (HEAD_DIM ** -0.5 * LOG2_E)).T.astype(BF16)

        pk = proj(Q_W, Q_W + KV_W)
        ssk = _mm((pk * pk).astype(BF16), bdk_ref[...])
        kn = pk * lax.rsqrt(ssk * (1.0 / HEAD_DIM) + EPS) * gk_ref[...]
        if not is_lat:
            kc_ref[0, rows, :] = kn
        for c in range(KV_W // LANES):
            kc = rope(kn[:, c * LANES:(c + 1) * LANES])
            rolled = pltpu.roll(kc, HEAD_DIM, 1)
            even = jnp.where(low_head, kc, rolled).astype(BF16)
            odd = jnp.where(low_head, rolled, kc).astype(BF16)
            for s in range(KV_W // LANES):
                krep_ref[0, 2 * c, rows, s * LANES:(s + 1) * LANES] = even
                krep_ref[0, 2 * c + 1, rows, s * LANES:(s + 1) * LANES] = odd

        pv = proj(Q_W + KV_W, Q_W + 2 * KV_W)
        if not is_lat:
            vc_ref[0, rows, :] = pv
        vt_ref[0, 0, :, rows] = pv.T.astype(BF16)

        o = Q_W + 2 * KV_W
        gb_ref[0, rows, :] = proj(o, o + CONV_W).astype(BF16)
        u_ref[0, rows, :] = proj(o + CONV_W, o + 2 * CONV_W) * proj(o + 2 * CONV_W, o + 3 * CONV_W)
        o = o + 3 * CONV_W
        sga_ref[0, rows, :] = jax.nn.sigmoid(proj(o, o + d)).astype(BF16)
        sgat_ref[0, rows, :] = jax.nn.sigmoid(proj(o + d, o + 2 * d)).astype(BF16)

    for c in range(tm // sub):
        row_chain(slice(c * sub, (c + 1) * sub))


def _premix(xa, xb, modp, mod_l, row_of_batch, lw, consts, is_lat, cast=()):
    b, n, d = xa.shape
    tm = min(MIX_TILE, n)
    tk = min(KEY_BLOCK, n)
    has_prev = xb is not None
    sub = tk // tm
    steps = n // tm
    tok_spec = lambda w: pl.BlockSpec((1, tm, w), lambda bi, i: (bi, i, 0))
    mod_spec = pl.BlockSpec((1, 1, 6 * d), lambda bi, i: (row_of_batch(bi), 0, 0))

    ins, specs = [xa], [tok_spec(d)]
    if has_prev:
        ins += [xb, modp]
        specs += [tok_spec(d), mod_spec]
    ins += [mod_l, lw["norm1"], lw["w_in"], lw["gq"], lw["gk"], consts["bdq"], consts["bdk"]]
    specs += [mod_spec, _const_spec((1, d)), _const_spec((d, IN_W), single=True), _const_spec((1, Q_W)),
              _const_spec((1, KV_W)), _const_spec((Q_W, Q_W), single=True), _const_spec((KV_W, KV_W))]
    if is_lat:
        ins += [consts["cos"], consts["sin"]]
        specs += [pl.BlockSpec((tm, LANES), lambda bi, i: (i, 0))] * 2

    outs, ospecs = [], []
    for w, layer in cast:
        _, wrows, wcols = w.shape
        slab = wrows // (b * steps)
        assert slab * b * steps == wrows and slab % SUBLANES == 0
        ins.append(w)
        specs.append(pl.BlockSpec((1, slab, wcols), lambda bi, i, layer=layer: (layer, bi * steps + i, 0)))
        outs.append(jax.ShapeDtypeStruct((wrows, wcols), BF16))
        ospecs.append(pl.BlockSpec((slab, wcols), lambda bi, i: (bi * steps + i, 0)))
    outs += [jax.ShapeDtypeStruct((b, Q_W, n), BF16),
             jax.ShapeDtypeStruct((b, N_KV_HEADS, n, KV_W), BF16),
             jax.ShapeDtypeStruct((b, n // tk, KV_W, tk), BF16)]
    ospecs += [pl.BlockSpec((1, Q_W, tm), lambda bi, i: (bi, 0, i)),
               pl.BlockSpec((1, N_KV_HEADS, tm, KV_W), lambda bi, i: (bi, 0, i, 0)),
               pl.BlockSpec((1, 1, KV_W, tm), lambda bi, i: (bi, i // sub, 0, i % sub))]
    if not is_lat:
        outs += [jax.ShapeDtypeStruct((b, n, KV_W), F32)] * 2
        ospecs += [tok_spec(KV_W)] * 2
    outs += [jax.ShapeDtypeStruct((b, n, CONV_W), F32), jax.ShapeDtypeStruct((b, n, CONV_W), BF16),
             jax.ShapeDtypeStruct((b, n, d), BF16), jax.ShapeDtypeStruct((b, n, d), BF16)]
    ospecs += [tok_spec(CONV_W), tok_spec(CONV_W), tok_spec(d), tok_spec(d)]
    if has_prev:
        outs.append(jax.ShapeDtypeStruct((b, n, d), F32))
        ospecs.append(tok_spec(d))

    res = pl.pallas_call(
        functools.partial(_premix_body, is_lat=is_lat, has_prev=has_prev, tm=tm, n_cast=len(cast)),
        out_shape=outs, grid=(b, n // tm), in_specs=specs, out_specs=ospecs,
        compiler_params=_cparams(("arbitrary", "arbitrary")),
        name="premix_lat" if is_lat else "premix_ctx",
    )(*ins)
    return list(res)


def _attn_body(*refs, p_len, n, tq, tk, hps):
    it = iter(refs)
    q_ref = next(it)
    if p_len:
        kc_ref = next(it)
        vc_ref = next(it)
    k_ref = next(it)
    v_ref = next(it)
    o_ref = next(it)
    qbd_ref = next(it)
    s_ref = next(it)
    m_ref = next(it)
    acc_ref = next(it)

    m_ref[...] = jnp.full(m_ref.shape, -jnp.inf, F32)
    acc_ref[...] = jnp.zeros(acc_ref.shape, F32)
    nb = n // tk

    def one_kv_head(hh):
        base = hh * GROUP
        vrows = slice(hh * HEAD_DIM, (hh + 1) * HEAD_DIM)
        lanes = slice(hh * KV_W, (hh + 1) * KV_W)

        for g in range(GROUP):
            band = q_ref[0, hh * KV_W + g * HEAD_DIM:hh * KV_W + (g + 1) * HEAD_DIM, :]
            pieces = [jnp.zeros((g * HEAD_DIM, tq), BF16), band, jnp.zeros(((GROUP - 1 - g) * HEAD_DIM, tq), BF16)]
            qbd_ref[base + g] = jnp.concatenate([piece for piece in pieces if piece.shape[0]], axis=0)

        def score(parts, g):
            off, m_blk = 0, None
            for get_k, _, size in parts:
                s = _mm(get_k(), qbd_ref[base + g])
                s_ref[base + g, off:off + size, :] = s
                m_part = jnp.max(s, axis=0, keepdims=True)
                m_blk = m_part if m_blk is None else jnp.maximum(m_blk, m_part)
                off += size
            return m_blk

        def softmax_pv(parts, g, m_blk):
            size = sum(part[2] for part in parts)
            m_prev = m_ref[base + g]
            m_new = jnp.maximum(m_prev, m_blk)
            alpha = jnp.exp2(m_prev - m_new)
            p = jnp.exp2(s_ref[base + g, 0:size, :] - m_new).astype(BF16)
            v_ext = jnp.concatenate([jnp.concatenate([get_v() for _, get_v, _ in parts], axis=1),
                                     jnp.ones((ONES_ROWS, size), BF16)], axis=0)
            acc_ref[base + g] = alpha * acc_ref[base + g] + _mm(v_ext, p)
            m_ref[base + g] = m_new

        def key_block(parts, m_blk, next_parts):
            for g in range(GROUP):
                if g + 1 < GROUP:
                    m_next = score(parts, g + 1)
                elif next_parts is not None:
                    m_next = score(next_parts, 0)
                else:
                    m_next = None
                softmax_pv(parts, g, m_blk)
                m_blk = m_next
            return m_blk

        def block_parts(i):
            if isinstance(i, int):
                parts = [(lambda: k_ref[0, hh, i * tk:(i + 1) * tk, :], lambda: v_ref[0, i, vrows, :], tk)]
                if p_len and i == 0:
                    parts.append((lambda: kc_ref[0, hh], lambda: vc_ref[0, vrows, :], p_len))
                return parts
            return [(lambda: k_ref[0, hh, pl.ds(pl.multiple_of(i * tk, tk), tk), :], lambda: v_ref[0, i, vrows, :], tk)]

        m_blk = score(block_parts(0), 0)
        if nb > 1:
            m_blk = key_block(block_parts(0), m_blk, block_parts(1))
            m_blk = lax.fori_loop(1, nb - 1, lambda i, m: key_block(block_parts(i), m, block_parts(i + 1)), m_blk)
        key_block(block_parts(nb - 1), m_blk, None)

        outs = []
        for g in range(GROUP):
            acc = acc_ref[base + g]
            outs.append(acc[0:HEAD_DIM, :] * (1.0 / acc[HEAD_DIM:HEAD_DIM + 1, :]))
        o_ref[0, lanes, :] = jnp.concatenate(outs, axis=0).astype(BF16)

    for hh in range(hps):
        one_kv_head(hh)


def _attention(q, krep, vt, kc_rep=None, vc_t=None):
    b, _, n = q.shape
    tk = vt.shape[-1]
    tq = min(QUERY_TILE, n)
    p_len = 0 if kc_rep is None else kc_rep.shape[2]
    hps = N_KV_HEADS if n == tk else 1
    ins, specs = [q], [pl.BlockSpec((1, hps * KV_W, tq), lambda bi, h, i: (bi, h, i))]
    if p_len:
        ins += [kc_rep, vc_t]
        specs += [pl.BlockSpec((1, hps, p_len, KV_W), lambda bi, h, i: (bi, h, 0, 0)),
                  pl.BlockSpec((1, hps * HEAD_DIM, p_len), lambda bi, h, i: (bi, h, 0))]
    ins += [krep, vt]
    specs += [pl.BlockSpec((1, hps, n, KV_W), lambda bi, h, i: (bi, h, 0, 0)),
              pl.BlockSpec((1, n // tk, hps * HEAD_DIM, tk), lambda bi, h, i: (bi, 0, h, 0))]
    return pl.pallas_call(
        functools.partial(_attn_body, p_len=p_len, n=n, tq=tq, tk=tk, hps=hps),
        out_shape=jax.ShapeDtypeStruct((b, Q_W, n), BF16),
        grid=(b, N_KV_HEADS // hps, n // tq),
        in_specs=specs,
        out_specs=pl.BlockSpec((1, hps * KV_W, tq), lambda bi, h, i: (bi, h, i)),
        scratch_shapes=[pltpu.VMEM((hps * GROUP, KV_W, tq), BF16),
                        pltpu.VMEM((hps * GROUP, tk + p_len, tq), F32),
                        pltpu.VMEM((hps * GROUP, 1, tq), F32),
                        pltpu.VMEM((hps * GROUP, HEAD_DIM + ONES_ROWS, tq), F32)],
        compiler_params=_cparams(("arbitrary", "arbitrary", "arbitrary")),
        name="attention_lat" if p_len else "attention_ctx",
    )(*ins)


def _split_bf16(x):
    hi = x.astype(BF16)
    lo = (x - hi.astype(F32)).astype(BF16)
    return hi, lo


def _postmix_body(attn_ref, u_ref, up_ref, un_ref, gb_ref, sga_ref, sgat_ref, x_ref, mod_ref, n2_ref, cw_ref,
                  wa_ref, wc_ref, wo_ref, wr_ref, x1_ref, hx_ref, afft_ref, *, tm):
    d = D_MODEL
    i = pl.program_id(1)
    last = pl.num_programs(1) - 1
    u = u_ref[0]
    row = lax.broadcasted_iota(I32, (tm, CONV_W), 0)
    prev_row = jnp.where(i > 0, up_ref[0, SUBLANES - 1:SUBLANES, :], 0.0)
    next_row = jnp.where(i < last, un_ref[0, 0:1, :], 0.0)
    u_m1 = jnp.where(row == 0, prev_row, pltpu.roll(u, 1, 0))
    u_p1 = jnp.where(row == tm - 1, next_row, pltpu.roll(u, tm - 1, 0))
    conv = cw_ref[0:1, :] * u_m1 + cw_ref[1:2, :] * u + cw_ref[2:3, :] * u_p1
    cv = (gb_ref[0].astype(F32) * conv).astype(BF16)
    r_hi, r_lo = _split_bf16(wr_ref[...])
    half = d // 2
    sub = min(ROW_CHAIN, tm)
    lane = lax.broadcasted_iota(I32, (sub, LANES), 1)
    valid = lane < N_EXPERTS

    for c in range(tm // sub):
        rows = slice(c * sub, (c + 1) * sub)
        conv_o = _mm(cv[rows], wc_ref[...])
        attn_o = _mm_tn(attn_ref[0, :, rows], wa_ref[...])
        merged = sga_ref[0, rows, :].astype(F32) * conv_o + sgat_ref[0, rows, :].astype(F32) * attn_o
        mix = _mm(merged.astype(BF16), wo_ref[...])
        x1 = x_ref[0, rows, :] + mod_ref[0, :, 2 * d:3 * d] * mix
        x1_ref[0, rows, :] = x1
        h2 = (_rms(x1) * n2_ref[...]) * (1.0 + mod_ref[0, :, 4 * d:5 * d]) + mod_ref[0, :, 3 * d:4 * d]

        h_hi, h_lo = _split_bf16(h2)
        w_lo = lax.bitcast_convert_type(h_hi[:, :half].astype(F32), U32) >> 16
        w_hi = lax.bitcast_convert_type(h_hi[:, half:].astype(F32), U32) & jnp.uint32(0xFFFF0000)
        hx_ref[0, rows, 0:half] = w_lo | w_hi

        logits = _mm(h_hi, r_hi)
        logits = jnp.where(valid, logits, -jnp.inf)
        e = jnp.exp(logits - jnp.max(logits, axis=-1, keepdims=True))
        e = jnp.where(valid, e, 0.0)
        aff = e / jnp.sum(e, axis=-1, keepdims=True)
        hx_ref[0, rows, half:half + LANES] = lax.bitcast_convert_type(aff, U32)
        afft_ref[0, :, rows] = aff.T[0:N_EXPERTS, :]


def _postmix(attn, u, gb, sga, sgat, x, mod_l, row_of_batch, lw):
    b, n, d = x.shape
    tm = min(MIX_TILE, n)
    nsub = n // SUBLANES
    per = tm // SUBLANES
    tok_spec = lambda w: pl.BlockSpec((1, tm, w), lambda bi, i: (bi, i, 0))
    specs = [pl.BlockSpec((1, Q_W, tm), lambda bi, i: (bi, 0, i)), tok_spec(CONV_W),
             pl.BlockSpec((1, SUBLANES, CONV_W), lambda bi, i: (bi, jnp.maximum(i * per - 1, 0), 0)),
             pl.BlockSpec((1, SUBLANES, CONV_W), lambda bi, i: (bi, jnp.minimum((i + 1) * per, nsub - 1), 0)),
             tok_spec(CONV_W), tok_spec(d), tok_spec(d), tok_spec(d),
             pl.BlockSpec((1, 1, 6 * d), lambda bi, i: (row_of_batch(bi), 0, 0)),
             _const_spec((1, d)), _const_spec((SUBLANES, CONV_W)),
             _const_spec((Q_W, d)), _const_spec((CONV_W, d)), _const_spec((d, d)), _const_spec((d, LANES))]
    outs = [jax.ShapeDtypeStruct((b, n, d), F32), jax.ShapeDtypeStruct((b, n, GATHER_W), U32),
            jax.ShapeDtypeStruct((b, N_EXPERTS, n), F32)]
    ospecs = [tok_spec(d), tok_spec(GATHER_W),
              pl.BlockSpec((1, N_EXPERTS, tm), lambda bi, i: (bi, 0, i))]
    return pl.pallas_call(
        functools.partial(_postmix_body, tm=tm),
        out_shape=outs, grid=(b, n // tm), in_specs=specs, out_specs=ospecs,
        compiler_params=_cparams(("arbitrary", "arbitrary")),
        name="postmix",
    )(attn, u, u, u, gb, sga, sgat, x, mod_l, lw["norm2"], lw["conv_w"], lw["w_attn_out"], lw["w_conv_out"],
      lw["w_o"], lw["w_router"])


def _topk_body(aff_ref, ut_ref, idx_ref, sel_ref, pos_ref, *, gb, n, cap):
    ne = N_EXPERTS
    rows_all = gb * ne
    a = aff_ref[...].reshape(rows_all, n)
    bits = lax.bitcast_convert_type(a, I32)

    def count(mask):
        return jnp.sum(jnp.where(mask, 1.0, 0.0), axis=1, keepdims=True)

    def thr_step(t, thr):
        cand = thr | (jnp.int32(1) << (30 - t))
        return jnp.where(count(bits >= cand) >= cap, cand, thr)

    thr = lax.fori_loop(0, 31, thr_step, jnp.zeros((rows_all, 1), I32))
    gt = bits > thr
    eq = bits == thr
    need = cap - count(gt)
    tok = lax.broadcasted_iota(I32, (rows_all, n), 1)
    nbits = int(np.log2(n)) + 1

    def tie_step(t, bound):
        cand = bound + (jnp.int32(1) << (nbits - 1 - t))
        ok = (cand <= n) & (count(eq & (tok < cand)) <= need)
        return jnp.where(ok, cand, bound)

    bound = lax.fori_loop(0, nbits, tie_step, jnp.zeros((rows_all, 1), I32))
    sel_ref[...] = jnp.where(gt | (eq & (tok < bound)), 1.0, 0.0)

    cb = min(CUMSUM_BLOCK, n)
    nblk = n // cb
    nrb = max(cap // RANK_BLOCK, 1)
    width = min(cap, RANK_BLOCK)
    tiles = RANK_BLOCK // SUBLANES
    lane = lax.broadcasted_iota(I32, (RANK_BLOCK, LANES), 1)
    rank0 = (lax.broadcasted_iota(I32, (tiles, SUBLANES, LANES), 0) * SUBLANES
             + lax.broadcasted_iota(I32, (tiles, SUBLANES, LANES), 1)).astype(F32)

    def batch_step(bi, carry):
        sel = sel_ref[pl.ds(pl.multiple_of(bi * ne, ne), ne), :]
        off = jnp.zeros((ne, 1), F32)
        for k in range(nblk):
            blk = _mm(sel[:, k * cb:(k + 1) * cb].astype(BF16), ut_ref[...])
            pos = blk + off
            for ex in range(ne):
                pos_ref[ex, :, k * cb:(k + 1) * cb] = jnp.broadcast_to(pos[ex:ex + 1, :], (SUBLANES, cb))
            off = off + blk[:, cb - 1:cb]

        def expert_step(ei, cols):
            for rb in range(nrb):
                rank = rank0 + float(rb * RANK_BLOCK)
                cnt = jnp.zeros((tiles, SUBLANES, LANES), F32)
                for lc in range(n // LANES):
                    prow = pos_ref[ei, :, lc * LANES:(lc + 1) * LANES]
                    cnt = cnt + jnp.where(prow[None] <= rank, 1.0, 0.0)
                col = jnp.sum(cnt.reshape(RANK_BLOCK, LANES), axis=1, keepdims=True)
                cols = jnp.where(lane == ei * nrb + rb, col, cols)
            return cols

        cols = lax.fori_loop(0, ne, expert_step, jnp.zeros((RANK_BLOCK, LANES), F32))
        rows = cols.T
        idx_ref[bi] = rows[0:ne * nrb, 0:width].astype(I32)
        return carry

    lax.fori_loop(0, gb, batch_step, 0)


def _topk(afft, consts):
    b, ne, n = afft.shape
    cap = max(1, EC_FACTOR * n // N_EXPERTS)
    nrb = max(cap // RANK_BLOCK, 1)
    width = min(cap, RANK_BLOCK)
    cb = min(CUMSUM_BLOCK, n)
    gb = min(b, max(1, TOPK_ROWS // (ne * n)))
    assert b % gb == 0
    idx = pl.pallas_call(
        functools.partial(_topk_body, gb=gb, n=n, cap=cap),
        out_shape=jax.ShapeDtypeStruct((b, ne * nrb, width), I32),
        grid=(b // gb,),
        in_specs=[pl.BlockSpec((gb, ne, n), lambda bi: (bi, 0, 0)), _const_spec((cb, cb))],
        out_specs=pl.BlockSpec((gb, ne * nrb, width), lambda bi: (bi, 0, 0)),
        scratch_shapes=[pltpu.VMEM((gb * ne, n), F32), pltpu.VMEM((ne, SUBLANES, n), F32)],
        compiler_params=_cparams(("arbitrary",)),
        name="topk",
    )(afft, consts["ut"][:cb, :cb])
    return idx.reshape(b, ne, cap)


def _ffn_body(idx_ref, idx_next_ref, hx_ref, wg_ref, wu_ref, wd_ref, y_ref, xs_ref, *, rows):
    e = pl.program_id(1)

    def gather_row(src_ref, slot, j, k):
        t = src_ref[0, 0, j * SUBLANES + k]
        xs_ref[slot, j, pl.ds(k, 1), :] = hx_ref[0, pl.ds(t, 1), :]

    @pl.when(e == 0)
    def _gather_first_expert():
        def body(j, carry):
            for k in range(SUBLANES):
                gather_row(idx_ref, 0, j, k)
            return carry
        lax.fori_loop(0, rows // SUBLANES, body, 0)

    slot = lax.rem(e, 2)
    half = D_MODEL // 2
    gathered = xs_ref[slot].reshape(rows, GATHER_W)
    for j in range(rows // SUBLANES):
        for k in range(SUBLANES):
            gather_row(idx_next_ref, 1 - slot, j, k)
    sub = min(ROW_CHAIN, rows)
    lane = lax.broadcasted_iota(I32, (sub, LANES), 1)
    for c in range(rows // sub):
        part = gathered[c * sub:(c + 1) * sub, :]
        words = part[:, 0:half]
        aff = lax.bitcast_convert_type(part[:, half:half + LANES], F32)
        x_lo = lax.bitcast_convert_type(words << 16, F32).astype(BF16)
        x_hi = lax.bitcast_convert_type(words & jnp.uint32(0xFFFF0000), F32).astype(BF16)

        def up_proj(w_ref):
            return (_mm(x_lo, w_ref[0, 0:half, :])
                    + _mm(x_hi, w_ref[0, half:, :]))

        gate = up_proj(wg_ref)
        hidden = ((gate * jax.nn.sigmoid(gate)) * up_proj(wu_ref)).astype(BF16)
        y = _mm(hidden, wd_ref[0])
        val = jnp.sum(jnp.where(lane == e, aff, 0.0), axis=1, keepdims=True)
        tile0 = c * (sub // SUBLANES)
        y_ref[0, 0, tile0:tile0 + sub // SUBLANES] = (y * val).reshape(sub // SUBLANES, SUBLANES, D_MODEL)


def _moe_ffn(idx_steps, hx, lw, group):
    ng, ne, rows = idx_steps.shape
    gtok = hx.shape[1] * group
    hx = hx.reshape(ng, gtok, GATHER_W)
    idx_flat = idx_steps.reshape(ng * ne, 1, rows)
    w_spec = lambda r, c: pl.BlockSpec((1, r, c), lambda g, e: (e, 0, 0))
    idx_spec = lambda step: pl.BlockSpec((1, 1, rows), lambda g, e: (g * ne + jnp.minimum(e + step, ne - 1), 0, 0),
                                         memory_space=pltpu.SMEM)
    return pl.pallas_call(
        functools.partial(_ffn_body, rows=rows),
        out_shape=jax.ShapeDtypeStruct((ng, ne, rows // SUBLANES, SUBLANES, D_MODEL), F32),
        grid=(ng, ne),
        in_specs=[idx_spec(0), idx_spec(1), pl.BlockSpec((1, gtok, GATHER_W), lambda g, e: (g, 0, 0)),
                  w_spec(D_MODEL, D_EXPERT), w_spec(D_MODEL, D_EXPERT), w_spec(D_EXPERT, D_MODEL)],
        out_specs=pl.BlockSpec((1, 1, rows // SUBLANES, SUBLANES, D_MODEL), lambda g, e: (g, e, 0, 0, 0)),
        scratch_shapes=[pltpu.VMEM((2, rows // SUBLANES, SUBLANES, GATHER_W), U32)],
        compiler_params=_cparams(("arbitrary", "arbitrary")),
        name="moe_ffn",
    )(idx_flat, idx_flat, hx, lw["w_gate"], lw["w_up"], lw["w_down"])


def _combine_body(idx_ref, y_ref, o_ref, *, rows):
    @pl.when(pl.program_id(1) == 0)
    def _zero():
        o_ref[...] = jnp.zeros(o_ref.shape, F32)

    def scatter(j, carry):
        toks = [idx_ref[0, 0, j * SUBLANES + k] for k in range(SUBLANES)]
        vals = [o_ref[0, pl.ds(toks[k], 1), :] + y_ref[0, 0, j, pl.ds(k, 1), :] for k in range(SUBLANES)]
        for k in range(SUBLANES):
            o_ref[0, pl.ds(toks[k], 1), :] = vals[k]
        return carry

    lax.fori_loop(0, rows // SUBLANES, scatter, 0)


def _combine(idx_steps, ysel, gtok):
    ng, ne, rows = idx_steps.shape
    return pl.pallas_call(
        functools.partial(_combine_body, rows=rows),
        out_shape=jax.ShapeDtypeStruct((ng, gtok, D_MODEL), F32),
        grid=(ng, ne),
        in_specs=[pl.BlockSpec((1, 1, rows), lambda g, e: (g * ne + e, 0, 0), memory_space=pltpu.SMEM),
                  pl.BlockSpec((1, 1, rows // SUBLANES, SUBLANES, D_MODEL), lambda g, e: (g, e, 0, 0, 0))],
        out_specs=pl.BlockSpec((1, gtok, D_MODEL), lambda g, e: (g, 0, 0)),
        compiler_params=_cparams(("arbitrary", "arbitrary")),
        name="moe_combine",
    )(idx_steps.reshape(ng * ne, 1, rows), ysel)


def _expert_choice(hx, afft, lw, consts, group):
    b, n, _ = hx.shape
    idx = _topk(afft, consts)
    cap = idx.shape[-1]
    ng = b // group
    offs = (jnp.arange(b, dtype=I32) % group * n).reshape(ng, group, 1, 1)
    idx_steps = (idx.reshape(ng, group, N_EXPERTS, cap) + offs).transpose(0, 2, 1, 3).reshape(ng, N_EXPERTS, group * cap)
    ysel = _moe_ffn(idx_steps, hx, lw, group)
    moe = _combine(idx_steps, ysel, group * n)
    return moe.reshape(b, n, D_MODEL)


def _final_body(x1_ref, moe_ref, mod_ref, g_ref, o_ref):
    d = D_MODEL
    x = x1_ref[0] + mod_ref[0, :, 5 * d:6 * d] * moe_ref[0]
    o_ref[0] = _rms(x) * g_ref[...]


def _final_norm(x1, moe, mod_l, row_of_batch, g):
    b, n, d = x1.shape
    tm = min(FINAL_TILE, n)
    tok = pl.BlockSpec((1, tm, d), lambda bi, i: (bi, i, 0))
    return pl.pallas_call(
        _final_body,
        out_shape=jax.ShapeDtypeStruct((b, n, d), F32),
        grid=(b, n // tm),
        in_specs=[tok, tok, pl.BlockSpec((1, 1, 6 * d), lambda bi, i: (row_of_batch(bi), 0, 0)), _const_spec((1, d))],
        out_specs=tok,
        compiler_params=_cparams(("arbitrary", "arbitrary")),
        name="final_norm",
    )(x1, moe, mod_l, g)


def _rope_tables(n_lat):
    half = AXIS_DIM // 2
    inv_freq = ROPE_THETA ** (-jnp.arange(half, dtype=F32) / half)
    t = jnp.arange(n_lat, dtype=jnp.int32)
    row = (t // GRID_W).astype(F32)
    col = (t % GRID_W).astype(F32)
    lane = np.arange(LANES)
    dim = lane % HEAD_DIM
    use_col = jnp.asarray(dim >= AXIS_DIM)
    freq = inv_freq[jnp.asarray(dim % half)]
    pos = jnp.where(use_col[None, :], col[:, None], row[:, None])
    ang = pos * freq[None, :]
    sign = jnp.asarray(np.where(dim % AXIS_DIM < half, -1.0, 1.0), dtype=F32)
    return jnp.cos(ang), jnp.sin(ang) * sign[None, :]


def _block_diag_ones(width):
    seg = np.arange(width) // HEAD_DIM
    return jnp.asarray(seg[:, None] == seg[None, :], dtype=BF16)


def kernel(x_prompt, x_sample, cache_k, cache_v, c, c_ctx, w_mod, b_mod, norm1, norm2, w_in, q_norm, k_norm,
           conv_w, w_conv_out, w_attn_out, w_o, w_router, w_gate, w_up, w_down, final_norm):
    depth = w_mod.shape[0]
    bc, nc, d = x_prompt.shape
    bl, nl, _ = x_sample.shape
    p_len = cache_k.shape[2]
    assert d == D_MODEL and bl + 1 <= MOD_ROWS
    assert nc % min(ROW_CHAIN, nc) == 0 and nl % KEY_BLOCK == 0 and p_len % LANES == 0

    cmat = jnp.zeros((MOD_ROWS, d), F32).at[0].set(c_ctx).at[1:1 + bl].set(c)
    mods = _modulation(cmat, w_mod, b_mod).reshape(depth, MOD_ROWS, 1, 6 * d)

    cos_t, sin_t = _rope_tables(nl)
    ut = np.arange(CUMSUM_BLOCK)
    consts = {"bdq": _block_diag_ones(Q_W), "bdk": _block_diag_ones(KV_W), "cos": cos_t, "sin": sin_t,
              "ut": jnp.asarray(ut[:, None] <= ut[None, :], dtype=BF16)}

    kc_rep = jnp.tile(cache_k.transpose(0, 1, 3, 2, 4), (1, 1, 1, 1, GROUP)).astype(BF16)
    vc_t = cache_v.transpose(0, 1, 3, 4, 2).reshape(bl, depth, KV_W, p_len).astype(BF16)

    ctx_row = lambda bi: 0
    lat_row = lambda bi: bi + 1
    ctx_group = min(CTX_GROUP, bc)

    expert_w = [w.reshape(depth, N_EXPERTS * w.shape[2], w.shape[3]) for w in (w_gate, w_up, w_down)]
    state = {"ctx": (x_prompt, None), "lat": (x_sample, None)}
    new_k, new_v = [], []
    for l in range(depth):
        lw = {
            "norm1": norm1[l].reshape(1, d), "norm2": norm2[l].reshape(1, d),
            "w_in": w_in[l].astype(BF16),
            "gq": jnp.tile(q_norm[l], N_HEADS).reshape(1, Q_W), "gk": jnp.tile(k_norm[l], N_KV_HEADS).reshape(1, KV_W),
            "conv_w": jnp.zeros((SUBLANES, CONV_W), F32).at[0:3].set(conv_w[l]),
            "w_conv_out": w_conv_out[l].astype(BF16), "w_attn_out": w_attn_out[l].astype(BF16),
            "w_o": w_o[l].astype(BF16),
            "w_router": jnp.zeros((d, LANES), F32).at[:, 0:N_EXPERTS].set(w_router[l]),
        }
        mod_l = mods[l]
        modp = mods[l - 1] if l else None
        xa, xb = state["lat"]
        lat_res = _premix(xa, xb, modp if xb is not None else None, mod_l, lat_row, lw, consts, True,
                          cast=[(w, l) for w in expert_w])
        for key, w16, w32 in zip(("w_gate", "w_up", "w_down"), lat_res[:3], (w_gate, w_up, w_down)):
            lw[key] = w16.reshape(w32.shape[1:])
        for name in ("ctx", "lat"):
            is_lat = name == "lat"
            row_fn = lat_row if is_lat else ctx_row
            xa, xb = state[name]
            if is_lat:
                res = lat_res[3:]
            else:
                res = _premix(xa, xb, modp if xb is not None else None, mod_l, row_fn, lw, consts, is_lat)
            q, krep, vt = res[0:3]
            pos = 3
            if not is_lat:
                new_k.append(res[3])
                new_v.append(res[4])
                pos = 5
            u, gb, sga, sgat = res[pos:pos + 4]
            x_cur = res[pos + 4] if xb is not None else xa
            if is_lat:
                attn = _attention(q, krep, vt, kc_rep[:, l], vc_t[:, l])
            else:
                attn = _attention(q, krep, vt)
            x1, hx, afft = _postmix(attn, u, gb, sga, sgat, x_cur, mod_l, row_fn, lw)
            moe = _expert_choice(hx, afft, lw, consts, group=1 if is_lat else ctx_group)
            state[name] = (x1, moe)

    g = final_norm.reshape(1, d)
    y_prompt = _final_norm(*state["ctx"], mods[depth - 1], ctx_row, g)
    y_sample = _final_norm(*state["lat"], mods[depth - 1], lat_row, g)
    shape_kv = (bc, depth, nc, N_KV_HEADS, HEAD_DIM)
    new_cache_k = jnp.stack(new_k, axis=1).reshape(shape_kv)
    new_cache_v = jnp.stack(new_v, axis=1).reshape(shape_kv)
    return (y_prompt, y_sample, new_cache_k, new_cache_v)
```
